```python
import jax, jax.numpy as jnp
from jax import lax
import numpy as np

D_MODEL = 1024
BATCH = 4
SEQ = 4096
DEPTH = 4

GRID_W = 64
CTX_LEN = 256
D_MIX = D_MODEL
D_LRU = D_MIX // 2
D_CONV = D_MIX - D_LRU
LRU_HEADS = 8
LRU_HEAD_DIM = D_LRU // LRU_HEADS
LRU_CONV_W = 4
LRU_CONV_PAD = (2, 1)
LRU_C = 8.0
CONF_CONV_W = 31
CONF_CONV_PAD = ((CONF_CONV_W - 1) // 2, (CONF_CONV_W - 1) // 2)
D_IN = 2 * D_LRU + 2 * D_CONV
D_FF = 2816
N_EXPERTS = 8
TOP_K = 2
N_DENSE = (DEPTH + 1) // 2
N_MOE = DEPTH // 2
N_MOD = 6
EPS = 1e-6

kernel_name = "hybrid_rglru_conformer_moe_dit"


def rmsnorm(x, g):
    xf = x.astype(jnp.float32)
    y = xf * lax.rsqrt(jnp.mean(xf * xf, axis=-1, keepdims=True) + EPS)
    return (y * g.astype(jnp.float32)).astype(x.dtype)


def layernorm(x, g, b):
    xf = x.astype(jnp.float32)
    mu = jnp.mean(xf, axis=-1, keepdims=True)
    var = jnp.mean(jnp.square(xf - mu), axis=-1, keepdims=True)
    y = (xf - mu) * lax.rsqrt(var + EPS)
    return (y * g.astype(jnp.float32) + b.astype(jnp.float32)).astype(x.dtype)


def modulate(h, shift, scale):
    return h * (1 + scale) + shift


def sincos_2d(rows, cols, dim):
    quarter = dim // 4
    omega = 1.0 / (10000.0 ** (jnp.arange(quarter, dtype=jnp.float32) / quarter))

    def emb1d(n):
        ang = jnp.arange(n, dtype=jnp.float32)[:, None] * omega[None, :]
        return jnp.concatenate([jnp.sin(ang), jnp.cos(ang)], axis=-1)

    er = jnp.broadcast_to(emb1d(rows)[:, None, :], (rows, cols, dim // 2))
    ec = jnp.broadcast_to(emb1d(cols)[None, :, :], (rows, cols, dim // 2))
    return jnp.concatenate([er, ec], axis=-1).reshape(rows * cols, dim)


def dwconv(x, w, b, pad):
    y = lax.conv_general_dilated(
        x, w[:, None, :].astype(x.dtype), window_strides=(1,), padding=[pad],
        dimension_numbers=("NWC", "WIO", "NWC"), feature_group_count=x.shape[-1])
    return y + b.astype(x.dtype)


def block_diag(x, w, b):
    xh = x.reshape(x.shape[:-1] + (LRU_HEADS, LRU_HEAD_DIM))
    return jnp.einsum("blhi,hij->blhj", xh, w).reshape(x.shape) + b


def rglru_coeffs(x, wa, ba, wx, bx, lam):
    r = jax.nn.sigmoid(block_diag(x, wa, ba)).astype(jnp.float32)
    i = jax.nn.sigmoid(block_diag(x, wx, bx)).astype(jnp.float32)
    log_a = -LRU_C * r * jax.nn.softplus(-lam.astype(jnp.float32))
    a = jnp.exp(log_a)
    b = jnp.sqrt(-jnp.expm1(2.0 * log_a)) * (i * x.astype(jnp.float32))
    return a, b


def linear_scan(a, b, h0, reverse):
    if h0 is not None:
        idx = -1 if reverse else 0
        b = b.at[:, idx].add(a[:, idx] * h0)

    def combine(left, right):
        a_l, b_l = left
        a_r, b_r = right
        return a_l * a_r, a_r * b_l + b_r

    _, h = lax.associative_scan(combine, (a, b), axis=1, reverse=reverse)
    return h


def conformer_branch(v, dw_w, dw_b, ln_g, ln_b):
    g = v[..., :D_CONV] * jax.nn.sigmoid(v[..., D_CONV:])
    g = dwconv(g, dw_w, dw_b, CONF_CONV_PAD)
    return jax.nn.silu(layernorm(g, ln_g, ln_b))


def mixer(h_c, h_x, need_ctx, w_in, w_out, conv_w, conv_b, wa, ba, wx, bx, lam,
          dw_w, dw_b, ln_g, ln_b):
    u_x = h_x @ w_in
    u_c = h_c @ w_in if need_ctx else h_c @ w_in[:, :D_LRU]
    r_x, gate_x, v_x = u_x[..., :D_LRU], u_x[..., D_LRU:2 * D_LRU], u_x[..., 2 * D_LRU:]
    r_c = dwconv(u_c[..., :D_LRU], conv_w, conv_b, LRU_CONV_PAD)
    r_x = dwconv(r_x, conv_w, conv_b, LRU_CONV_PAD)
    y_x = None
    y_c = None
    for d, rev in enumerate((False, True)):
        a, b = rglru_coeffs(r_c, wa[d], ba[d], wx[d], bx[d], lam[d])
        hc = linear_scan(a, b, None, rev)
        h_end = hc[:, 0] if rev else hc[:, -1]
        a, b = rglru_coeffs(r_x, wa[d], ba[d], wx[d], bx[d], lam[d])
        hx = linear_scan(a, b, h_end, rev)
        y_x = hx if y_x is None else y_x + hx
        if need_ctx:
            y_c = hc if y_c is None else y_c + hc

    def merge(y, gate, v):
        lru = y.astype(gate.dtype) * jax.nn.gelu(gate)
        conf = conformer_branch(v, dw_w, dw_b, ln_g, ln_b)
        return jnp.concatenate([lru, conf], axis=-1) @ w_out

    out_x = merge(y_x, gate_x, v_x)
    out_c = merge(y_c, u_c[..., D_LRU:2 * D_LRU], u_c[..., 2 * D_LRU:]) if need_ctx else None
    return out_c, out_x


def swiglu(h, w1, w3, w2):
    return (jax.nn.silu(h @ w1) * (h @ w3)) @ w2


def moe_swiglu(h, w_router, w1, w3, w2):
    logits = (h @ w_router).astype(jnp.float32)
    probs = jax.nn.softmax(logits, axis=-1)
    top_p, top_i = lax.top_k(probs, TOP_K)
    top_p = top_p / jnp.sum(top_p, axis=-1, keepdims=True)
    comb = jnp.sum(jax.nn.one_hot(top_i, N_EXPERTS, dtype=jnp.float32) * top_p[..., None], axis=-2)
    comb = comb.astype(h.dtype)
    y = jnp.zeros_like(h)
    for e in range(N_EXPERTS):
        y = y + comb[..., e:e + 1] * swiglu(h, w1[e], w3[e], w2[e])
    return y


def setup_inputs(seed: int = 0) -> dict:
    key = jax.random.key(seed)
    ks = jax.random.split(key, 32)

    def nrm(k, shape, scale):
        return jax.random.normal(k, shape, jnp.float32) * scale

    x = nrm(ks[0], (BATCH, SEQ, D_MODEL), 1.0)
    c = nrm(ks[1], (BATCH, D_MODEL), 1.0)
    ctx = nrm(ks[2], (BATCH, CTX_LEN, D_MODEL), 1.0)
    c_ctx = nrm(ks[3], (D_MODEL,), 1.0)
    w_mod = nrm(ks[4], (DEPTH, D_MODEL, N_MOD * D_MODEL), 0.5 * D_MODEL ** -0.5)
    b_mod = nrm(ks[5], (DEPTH, N_MOD * D_MODEL), 0.02)
    norm1_g = 1.0 + nrm(ks[6], (DEPTH, D_MODEL), 0.02)
    norm2_g = 1.0 + nrm(ks[7], (DEPTH, D_MODEL), 0.02)
    w_in = nrm(ks[8], (DEPTH, D_MODEL, D_IN), D_MODEL ** -0.5)
    w_out = nrm(ks[9], (DEPTH, D_MIX, D_MODEL), D_MIX ** -0.5)
    lru_conv_w = nrm(ks[10], (DEPTH, LRU_CONV_W, D_LRU), LRU_CONV_W ** -0.5)
    lru_conv_b = nrm(ks[11], (DEPTH, D_LRU), 0.02)
    lru_wa = nrm(ks[12], (DEPTH, 2, LRU_HEADS, LRU_HEAD_DIM, LRU_HEAD_DIM), LRU_HEAD_DIM ** -0.5)
    lru_ba = nrm(ks[13], (DEPTH, 2, D_LRU), 0.02)
    lru_wx = nrm(ks[14], (DEPTH, 2, LRU_HEADS, LRU_HEAD_DIM, LRU_HEAD_DIM), LRU_HEAD_DIM ** -0.5)
    lru_bx = nrm(ks[15], (DEPTH, 2, D_LRU), 0.02)
    a_c = jax.random.uniform(ks[16], (DEPTH, 2, D_LRU), jnp.float32, 0.9, 0.999)
    s = a_c ** (1.0 / LRU_C)
    lru_lambda = jnp.log(s) - jnp.log1p(-s)
    conf_dw_w = nrm(ks[17], (DEPTH, CONF_CONV_W, D_CONV), CONF_CONV_W ** -0.5)
    conf_dw_b = nrm(ks[18], (DEPTH, D_CONV), 0.02)
    conf_ln_g = 1.0 + nrm(ks[19], (DEPTH, D_CONV), 0.02)
    conf_ln_b = nrm(ks[20], (DEPTH, D_CONV), 0.02)
    ffn_w1 = nrm(ks[21], (N_DENSE, D_MODEL, D_FF), D_MODEL ** -0.5)
    ffn_w3 = nrm(ks[22], (N_DENSE, D_MODEL, D_FF), D_MODEL ** -0.5)
    ffn_w2 = nrm(ks[23], (N_DENSE, D_FF, D_MODEL), D_FF ** -0.5)
    moe_router = nrm(ks[24], (N_MOE, D_MODEL, N_EXPERTS), D_MODEL ** -0.5)
    moe_w1 = nrm(ks[25], (N_MOE, N_EXPERTS, D_MODEL, D_FF), D_MODEL ** -0.5)
    moe_w3 = nrm(ks[26], (N_MOE, N_EXPERTS, D_MODEL, D_FF), D_MODEL ** -0.5)
    moe_w2 = nrm(ks[27], (N_MOE, N_EXPERTS, D_FF, D_MODEL), D_FF ** -0.5)
    final_g = 1.0 + nrm(ks[28], (D_MODEL,), 0.02)
    return {"x": x, "c": c, "ctx": ctx, "c_ctx": c_ctx, "w_mod": w_mod, "b_mod": b_mod,
            "norm1_g": norm1_g, "norm2_g": norm2_g, "w_in": w_in, "w_out": w_out,
            "lru_conv_w": lru_conv_w, "lru_conv_b": lru_conv_b, "lru_wa": lru_wa, "lru_ba": lru_ba,
            "lru_wx": lru_wx, "lru_bx": lru_bx, "lru_lambda": lru_lambda,
            "conf_dw_w": conf_dw_w, "conf_dw_b": conf_dw_b, "conf_ln_g": conf_ln_g, "conf_ln_b": conf_ln_b,
            "ffn_w1": ffn_w1, "ffn_w3": ffn_w3, "ffn_w2": ffn_w2, "moe_router": moe_router,
            "moe_w1": moe_w1, "moe_w3": moe_w3, "moe_w2": moe_w2, "final_g": final_g}


def reference(x, c, ctx, c_ctx, w_mod, b_mod, norm1_g, norm2_g, w_in, w_out,
              lru_conv_w, lru_conv_b, lru_wa, lru_ba, lru_wx, lru_bx, lru_lambda,
              conf_dw_w, conf_dw_b, conf_ln_g, conf_ln_b,
              ffn_w1, ffn_w3, ffn_w2, moe_router, moe_w1, moe_w3, moe_w2, final_g):
    rows = x.shape[1] // GRID_W
    x = x + sincos_2d(rows, GRID_W, D_MODEL).astype(x.dtype)
    xc = ctx
    n_ctx = ctx.shape[1]
    for l in range(DEPTH):
        last = l == DEPTH - 1
        m_x = jax.nn.silu(c) @ w_mod[l] + b_mod[l]
        m_c = jax.nn.silu(c_ctx) @ w_mod[l] + b_mod[l]
        sh1x, sc1x, g1x, sh2x, sc2x, g2x = [t[:, None, :] for t in jnp.split(m_x, N_MOD, axis=-1)]
        sh1c, sc1c, g1c, sh2c, sc2c, g2c = jnp.split(m_c, N_MOD, axis=-1)

        h_x = modulate(rmsnorm(x, norm1_g[l]), sh1x, sc1x)
        h_c = modulate(rmsnorm(xc, norm1_g[l]), sh1c, sc1c)
        o_c, o_x = mixer(h_c, h_x, not last, w_in[l], w_out[l], lru_conv_w[l], lru_conv_b[l],
                         lru_wa[l], lru_ba[l], lru_wx[l], lru_bx[l], lru_lambda[l],
                         conf_dw_w[l], conf_dw_b[l], conf_ln_g[l], conf_ln_b[l])
        x = x + g1x * o_x

        f_x = modulate(rmsnorm(x, norm2_g[l]), sh2x, sc2x)
        if last:
            h = f_x
        else:
            xc = xc + g1c * o_c
            f_c = modulate(rmsnorm(xc, norm2_g[l]), sh2c, sc2c)
            h = jnp.concatenate([f_c, f_x], axis=1)
        if l % 2 == 0:
            j = l // 2
            y = swiglu(h, ffn_w1[j], ffn_w3[j], ffn_w2[j])
        else:
            j = l // 2
            y = moe_swiglu(h, moe_router[j], moe_w1[j], moe_w3[j], moe_w2[j])
        if last:
            x = x + g2x * y
        else:
            xc = xc + g2c * y[:, :n_ctx]
            x = x + g2x * y[:, n_ctx:]
    return rmsnorm(x, final_g)
```

```python
import functools
import math

import jax
import jax.numpy as jnp
from jax import lax
from jax.experimental import pallas as pl
from jax.experimental.pallas import tpu as pltpu

F32 = jnp.float32
BF16 = jnp.bfloat16

D_MODEL = 1024
DEPTH = 4
GRID_W = 64
D_LRU = 512
D_CONV = 512
LRU_HEADS = 8
LRU_HEAD_DIM = 64
LRU_CONV_W = 4
LRU_C = 8.0
CONF_CONV_W = 31
D_FF = 2816
N_EXPERTS = 8
N_MOD = 6
EPS = 1e-6

LANES = 128
SUBLANES = 8
BF16_ROWS = 16
MXU_DIM = 256
VMEM_LIMIT = 56 * 1024 * 1024

HALO = BF16_ROWS
CONV_CHUNK = 32
CTX_MOD_ROW = 4


def _silu(v):
    return v * jax.nn.sigmoid(v)


def _rms_mod(x, g, shift, scale):
    ms = jnp.mean(x * x, axis=-1, keepdims=True)
    y = x * lax.rsqrt(ms + EPS) * g
    return y * (1.0 + scale) + shift


def _mod_row(mod_ref, row, k):
    r = mod_ref[pl.ds(row, 1), :]
    return r[:, k * D_MODEL:(k + 1) * D_MODEL]


def _mod_kernel(cc_ref, w_ref, b_ref, o_ref):
    s = _silu(cc_ref[...])
    o_ref[0] = jnp.dot(s.astype(BF16), w_ref[0].astype(BF16),
                       preferred_element_type=F32) + b_ref[0]


def _modulation(cc, w_mod, b_mod):
    n_out = N_MOD * D_MODEL
    tn = 1536
    return pl.pallas_call(
        _mod_kernel,
        grid=(DEPTH, n_out // tn),
        in_specs=[
            pl.BlockSpec((SUBLANES, D_MODEL), lambda l, j: (0, 0)),
            pl.BlockSpec((1, D_MODEL, tn), lambda l, j: (l, 0, j)),
            pl.BlockSpec((1, 1, tn), lambda l, j: (l, 0, j)),
        ],
        out_specs=pl.BlockSpec((1, SUBLANES, tn), lambda l, j: (l, 0, j)),
        out_shape=jax.ShapeDtypeStruct((DEPTH, SUBLANES, n_out), F32),
        compiler_params=pltpu.CompilerParams(
            dimension_semantics=("arbitrary", "arbitrary"),
            vmem_limit_bytes=VMEM_LIMIT),
        name="modulation",
    )(cc, w_mod, b_mod.reshape(DEPTH, 1, n_out))


def _in_kernel(*refs, add_pos, mod_row):
    if add_pos:
        x_ref, pos_ref, g_ref, mod_ref, w_ref, u_ref, xo_ref = refs
    else:
        x_ref, g_ref, mod_ref, w_ref, u_ref = refs
    row = pl.program_id(0) if mod_row is None else mod_row
    x = x_ref[0]
    if add_pos:
        x = x + pos_ref[...]
        xo_ref[0] = x
    h = _rms_mod(x, g_ref[...], _mod_row(mod_ref, row, 0), _mod_row(mod_ref, row, 1))
    u_ref[0] = jnp.dot(h.astype(BF16), w_ref[...],
                       preferred_element_type=F32).astype(BF16)


def _in_proj(x, pos, g, mod, w, *, tm, mod_row):
    nb, s, d = x.shape
    n = w.shape[1]
    add_pos = pos is not None
    in_specs = [pl.BlockSpec((1, tm, d), lambda b, t: (b, t, 0))]
    args = [x]
    if add_pos:
        in_specs.append(pl.BlockSpec((tm, d), lambda b, t: (t, 0)))
        args.append(pos)
    in_specs += [
        pl.BlockSpec((1, d), lambda b, t: (0, 0)),
        pl.BlockSpec((SUBLANES, N_MOD * d), lambda b, t: (0, 0)),
        pl.BlockSpec((d, n), lambda b, t: (0, 0)),
    ]
    args += [g.reshape(1, d), mod, w]
    out_specs = [pl.BlockSpec((1, tm, n), lambda b, t: (b, t, 0))]
    out_shape = [jax.ShapeDtypeStruct((nb, s, n), BF16)]
    if add_pos:
        out_specs.append(pl.BlockSpec((1, tm, d), lambda b, t: (b, t, 0)))
        out_shape.append(jax.ShapeDtypeStruct((nb, s, d), F32))
    res = pl.pallas_call(
        functools.partial(_in_kernel, add_pos=add_pos, mod_row=mod_row),
        grid=(nb, s // tm),
        in_specs=in_specs,
        out_specs=out_specs,
        out_shape=out_shape,
        compiler_params=pltpu.CompilerParams(
            dimension_semantics=("arbitrary", "arbitrary"),
            vmem_limit_bytes=VMEM_LIMIT),
        name="in_proj",
    )(*args)
    return (res[0], res[1]) if add_pos else (res[0], x)


def _lru_coeffs(ext_ref, u_ref, prev_ref, next_ref, pvec_ref, wg_ref, has_prev, has_next, ts):
    prev = prev_ref[0].astype(F32)
    nxt = next_ref[0].astype(F32)
    ext_ref[0:SUBLANES, :] = prev[HALO - SUBLANES:HALO, :] * has_prev
    ext_ref[SUBLANES:SUBLANES + ts, :] = u_ref[0].astype(F32)
    ext_ref[SUBLANES + ts:2 * SUBLANES + ts, :] = nxt[0:SUBLANES, :] * has_next
    rc = pvec_ref[4:5, :]
    for k in range(LRU_CONV_W):
        rc = rc + pvec_ref[k:k + 1, :] * ext_ref[pl.ds(SUBLANES - 2 + k, ts), :]
    rcb = rc.astype(BF16)
    ga, gx = [], []
    for j in range(D_LRU // MXU_DIM):
        o = jnp.dot(rcb[:, j * MXU_DIM:(j + 1) * MXU_DIM], wg_ref[j],
                    preferred_element_type=F32)
        ga.append(o[:, :MXU_DIM])
        gx.append(o[:, MXU_DIM:])
    r_gate = jax.nn.sigmoid(jnp.concatenate(ga, axis=-1) + pvec_ref[5:6, :])
    i_gate = jax.nn.sigmoid(jnp.concatenate(gx, axis=-1) + pvec_ref[6:7, :])
    neg_lam = -pvec_ref[7:8, :]
    softplus = jnp.maximum(neg_lam, 0.0) + jnp.log1p(jnp.exp(-jnp.abs(neg_lam)))
    log_a = -LRU_C * r_gate * softplus
    a = jnp.exp(log_a)
    one_minus_a2 = -jnp.tanh(log_a) * (a * a + 1.0)
    b = jnp.sqrt(one_minus_a2) * (i_gate * rc)
    return a, b


def _lru_scan(a, b, h_in, hs_ref, ts, reverse):
    groups = ts // SUBLANES
    c = a.shape[-1]
    a3 = a.reshape(groups, SUBLANES, c)
    b3 = b.reshape(groups, SUBLANES, c)
    row = lax.broadcasted_iota(jnp.int32, (groups, SUBLANES, c), 1)
    for s in (1, 2, 4):
        shift = SUBLANES - s if reverse else s
        a_sh = pltpu.roll(a3, shift, axis=1)
        b_sh = pltpu.roll(b3, shift, axis=1)
        m = (row < SUBLANES - s) if reverse else (row >= s)
        b3 = jnp.where(m, a3 * b_sh + b3, b3)
        a3 = jnp.where(m, a3 * a_sh, a3)
    h = h_in
    order = range(groups - 1, -1, -1) if reverse else range(groups)
    edge = 0 if reverse else SUBLANES - 1
    for g in order:
        hg = a3[g] * h + b3[g]
        hs_ref[g * SUBLANES:(g + 1) * SUBLANES, :] = hg
        h = jnp.broadcast_to(hg[edge:edge + 1, :], (SUBLANES, c))
    return h


def _halo_specs(width, col, ts, s, tile_of):
    per = ts // HALO
    last = s // HALO - 1
    main = pl.BlockSpec((1, ts, width), lambda b, t: (b, tile_of(t), col))
    prev = pl.BlockSpec((1, HALO, width),
                        lambda b, t: (b, jnp.maximum(tile_of(t) * per - 1, 0), col))
    nxt = pl.BlockSpec((1, HALO, width),
                       lambda b, t: (b, jnp.minimum((tile_of(t) + 1) * per, last), col))
    return [main, prev, nxt]


def _fwd_kernel(u_ref, prev_ref, next_ref, pvec_ref, wg_ref, seed_ref,
                hf_ref, state_ref, ext_ref, hs_ref, carry_ref, *, ts):
    t = pl.program_id(1)
    nt = pl.num_programs(1)

    @pl.when(t == 0)
    def _():
        carry_ref[...] = seed_ref[0]

    has_prev = (t > 0).astype(F32)
    has_next = (t < nt - 1).astype(F32)
    a, b = _lru_coeffs(ext_ref, u_ref, prev_ref, next_ref, pvec_ref, wg_ref,
                       has_prev, has_next, ts)
    h = _lru_scan(a, b, carry_ref[...], hs_ref, ts, reverse=False)
    carry_ref[...] = h
    state_ref[0] = h
    hf_ref[0] = hs_ref[...].astype(BF16)


def _lru_forward(u, pvec, wg, seed, *, ts):
    nb, s, _ = u.shape
    return pl.pallas_call(
        functools.partial(_fwd_kernel, ts=ts),
        grid=(nb, s // ts),
        in_specs=_halo_specs(D_LRU, 0, ts, s, lambda t: t) + [
            pl.BlockSpec((2 * SUBLANES, D_LRU), lambda b, t: (0, 0)),
            pl.BlockSpec((D_LRU // MXU_DIM, MXU_DIM, 2 * MXU_DIM), lambda b, t: (0, 0, 0)),
            pl.BlockSpec((1, SUBLANES, D_LRU), lambda b, t: (b, 0, 0)),
        ],
        out_specs=[
            pl.BlockSpec((1, ts, D_LRU), lambda b, t: (b, t, 0)),
            pl.BlockSpec((1, SUBLANES, D_LRU), lambda b, t: (b, 0, 0)),
        ],
        out_shape=[
            jax.ShapeDtypeStruct((nb, s, D_LRU), BF16),
            jax.ShapeDtypeStruct((nb, SUBLANES, D_LRU), F32),
        ],
        scratch_shapes=[
            pltpu.VMEM((ts + 2 * SUBLANES, D_LRU), F32),
            pltpu.VMEM((ts, D_LRU), F32),
            pltpu.VMEM((SUBLANES, D_LRU), F32),
        ],
        compiler_params=pltpu.CompilerParams(
            dimension_semantics=("arbitrary", "arbitrary"),
            vmem_limit_bytes=VMEM_LIMIT),
        name="lru_forward",
    )(u, u, u, pvec, wg, seed)


def _bwd_kernel(*refs, ts, merge, mod_row):
    if merge:
        (u_ref, prev_ref, next_ref, pvec_ref, wg_ref, seed_ref,
         gate_ref, v_ref, vprev_ref, vnext_ref, hf_ref, x_ref, mod_ref,
         dww_ref, cvec_ref, wo_ref,
         xo_ref, state_ref,
         ext_ref, hs_ref, carry_ref, g_ref, cs_ref, sh_ref) = refs
    else:
        (u_ref, prev_ref, next_ref, pvec_ref, wg_ref, seed_ref,
         state_ref, ext_ref, hs_ref, carry_ref) = refs
    tg = pl.program_id(1)
    nt = pl.num_programs(1)
    t = nt - 1 - tg

    @pl.when(tg == 0)
    def _():
        carry_ref[...] = seed_ref[0]

    has_prev = (t > 0).astype(F32)
    has_next = (t < nt - 1).astype(F32)
    a, b = _lru_coeffs(ext_ref, u_ref, prev_ref, next_ref, pvec_ref, wg_ref,
                       has_prev, has_next, ts)
    h = _lru_scan(a, b, carry_ref[...], hs_ref, ts, reverse=True)
    carry_ref[...] = h
    state_ref[0] = h
    if not merge:
        return

    y = hf_ref[0].astype(F32) + hs_ref[...]
    gate = gate_ref[0].astype(F32)
    cdf = 0.5 * (1.0 + jnp.tanh(math.sqrt(2.0 / math.pi) * (gate + 0.044715 * (gate * gate * gate))))
    lru = (y * (gate * cdf)).astype(BF16)

    def glu(vref):
        v = vref[0].astype(F32)
        return v[:, :D_CONV] * jax.nn.sigmoid(v[:, D_CONV:])

    g_ref[0:HALO, :] = glu(vprev_ref) * has_prev
    g_ref[HALO:HALO + ts, :] = glu(v_ref)
    g_ref[HALO + ts:2 * HALO + ts, :] = glu(vnext_ref) * has_next
    pad = (CONF_CONV_W - 1) // 2
    dwb = cvec_ref[0:1, :]
    span = ts + 2 * HALO - SUBLANES
    for r in range(1, SUBLANES):
        sh_ref[r - 1, 0:span, :] = g_ref[pl.ds(r, span), :]

    def conv_chunk(i, carry):
        base = pl.multiple_of(i * CONV_CHUNK, CONV_CHUNK)
        acc = jnp.zeros((CONV_CHUNK, D_CONV), F32) + dwb
        for k in range(CONF_CONV_W):
            q, r = divmod(HALO - pad + k, SUBLANES)
            rows = pl.ds(base + q * SUBLANES, CONV_CHUNK)
            tap = g_ref[rows, :] if r == 0 else sh_ref[r - 1, rows, :]
            acc = acc + dww_ref[k:k + 1, :] * tap
        cs_ref[pl.ds(base, CONV_CHUNK), :] = acc
        return carry

    lax.fori_loop(0, ts // CONV_CHUNK, conv_chunk, 0)
    cv = cs_ref[...]
    mu = jnp.mean(cv, axis=-1, keepdims=True)
    var = jnp.mean(jnp.square(cv - mu), axis=-1, keepdims=True)
    ln = (cv - mu) * lax.rsqrt(var + EPS) * cvec_ref[1:2, :] + cvec_ref[2:3, :]
    conf = _silu(ln).astype(BF16)

    o = jnp.dot(lru, wo_ref[0:D_LRU, :], preferred_element_type=F32)
    o = o + jnp.dot(conf, wo_ref[D_LRU:, :], preferred_element_type=F32)
    row = pl.program_id(0) if mod_row is None else mod_row
    xo_ref[0] = x_ref[0] + _mod_row(mod_ref, row, 2) * o


def _lru_backward(u, pvec, wg, seed, *, ts, merge_args=None, mod_row=None):
    nb, s, _ = u.shape
    nt = s // ts
    rev = lambda t: nt - 1 - t
    in_specs = _halo_specs(D_LRU, 0, ts, s, rev) + [
        pl.BlockSpec((2 * SUBLANES, D_LRU), lambda b, t: (0, 0)),
        pl.BlockSpec((D_LRU // MXU_DIM, MXU_DIM, 2 * MXU_DIM), lambda b, t: (0, 0, 0)),
        pl.BlockSpec((1, SUBLANES, D_LRU), lambda b, t: (b, 0, 0)),
    ]
    args = [u, u, u, pvec, wg, seed]
    state_spec = pl.BlockSpec((1, SUBLANES, D_LRU), lambda b, t: (b, 0, 0))
    state_shape = jax.ShapeDtypeStruct((nb, SUBLANES, D_LRU), F32)
    scratch = [
        pltpu.VMEM((ts + 2 * SUBLANES, D_LRU), F32),
        pltpu.VMEM((ts, D_LRU), F32),
        pltpu.VMEM((SUBLANES, D_LRU), F32),
    ]
    merge = merge_args is not None
    if merge:
        hf, x, mod, dww, cvec, wo = merge_args
        d = x.shape[-1]
        in_specs += [pl.BlockSpec((1, ts, D_LRU), lambda b, t: (b, rev(t), 1))]
        in_specs += _halo_specs(2 * D_CONV, 1, ts, s, rev)
        in_specs += [
            pl.BlockSpec((1, ts, D_LRU), lambda b, t: (b, rev(t), 0)),
            pl.BlockSpec((1, ts, d), lambda b, t: (b, rev(t), 0)),
            pl.BlockSpec((SUBLANES, N_MOD * d), lambda b, t: (0, 0)),
            pl.BlockSpec((4 * SUBLANES, D_CONV), lambda b, t: (0, 0)),
            pl.BlockSpec((SUBLANES, D_CONV), lambda b, t: (0, 0)),
            pl.BlockSpec((d, d), lambda b, t: (0, 0)),
        ]
        args += [u, u, u, u, hf, x, mod, dww, cvec, wo]
        out_specs = [pl.BlockSpec((1, ts, d), lambda b, t: (b, rev(t), 0)), state_spec]
        out_shape = [jax.ShapeDtypeStruct((nb, s, d), F32), state_shape]
        scratch += [
            pltpu.VMEM((ts + 2 * HALO, D_CONV), F32),
            pltpu.VMEM((ts, D_CONV), F32),
            pltpu.VMEM((SUBLANES - 1, ts + 2 * HALO, D_CONV), F32),
        ]
    else:
        out_specs = [state_spec]
        out_shape = [state_shape]
    res = pl.pallas_call(
        functools.partial(_bwd_kernel, ts=ts, merge=merge, mod_row=mod_row),
        grid=(nb, nt),
        in_specs=in_specs,
        out_specs=out_specs,
        out_shape=out_shape,
        scratch_shapes=scratch,
        compiler_params=pltpu.CompilerParams(
            dimension_semantics=("arbitrary", "arbitrary"),
            vmem_limit_bytes=VMEM_LIMIT),
        name="lru_backward_merge" if merge else "lru_backward",
    )(*args)
    return res if merge else (None, res[0])


def _router_kernel(x_ref, g_ref, mod_ref, wr_ref, comb_ref, *, tiles_per_batch, mod_row):
    i = pl.program_id(0)
    row = i // tiles_per_batch if mod_row is None else mod_row
    h = _rms_mod(x_ref[...], g_ref[...], _mod_row(mod_ref, row, 3), _mod_row(mod_ref, row, 4))
    logits = jnp.dot(h, wr_ref[...], preferred_element_type=F32,
                     precision=lax.Precision.HIGHEST)
    lane = lax.broadcasted_iota(jnp.int32, logits.shape, 1).astype(F32)
    neg = jnp.float32(-jnp.inf)
    l1 = jnp.where(lane < N_EXPERTS, logits, neg)
    m1 = jnp.max(l1, axis=-1, keepdims=True)
    i1 = jnp.min(jnp.where(l1 == m1, lane, float(LANES)), axis=-1, keepdims=True)
    l2 = jnp.where(lane == i1, neg, l1)
    m2 = jnp.max(l2, axis=-1, keepdims=True)
    i2 = jnp.min(jnp.where(l2 == m2, lane, float(LANES)), axis=-1, keepdims=True)
    e = jnp.exp(m2 - m1)
    p1 = 1.0 / (1.0 + e)
    p2 = e / (1.0 + e)
    comb_ref[...] = jnp.where(lane == i1, p1, 0.0) + jnp.where(lane == i2, p2, 0.0)


def _router(x2, g, mod, wr, *, tm, tiles_per_batch, mod_row):
    t, d = x2.shape
    return pl.pallas_call(
        functools.partial(_router_kernel, tiles_per_batch=tiles_per_batch, mod_row=mod_row),
        grid=(t // tm,),
        in_specs=[
            pl.BlockSpec((tm, d), lambda i: (i, 0)),
            pl.BlockSpec((1, d), lambda i: (0, 0)),
            pl.BlockSpec((SUBLANES, N_MOD * d), lambda i: (0, 0)),
            pl.BlockSpec((d, LANES), lambda i: (0, 0)),
        ],
        out_specs=pl.BlockSpec((tm, LANES), lambda i: (i, 0)),
        out_shape=jax.ShapeDtypeStruct((t, LANES), F32),
        compiler_params=pltpu.CompilerParams(
            dimension_semantics=("arbitrary",),
            vmem_limit_bytes=VMEM_LIMIT),
        name="router",
    )(x2, g.reshape(1, d), mod, wr)


def _ffn_kernel(*refs, moe, final, tiles_per_batch, mod_row):
    if moe:
        x_ref, comb_ref, g_ref, fg_ref, mod_ref, w1_ref, w3_ref, w2_ref, o_ref, h_ref, acc_ref = refs
    else:
        x_ref, g_ref, fg_ref, mod_ref, w1_ref, w3_ref, w2_ref, o_ref, h_ref, acc_ref = refs
    i = pl.program_id(0)
    e = pl.program_id(1)
    f = pl.program_id(2)
    row = i // tiles_per_batch if mod_row is None else mod_row

    @pl.when((e == 0) & (f == 0))
    def _():
        h = _rms_mod(x_ref[...], g_ref[...], _mod_row(mod_ref, row, 3), _mod_row(mod_ref, row, 4))
        h_ref[...] = h.astype(BF16)
        acc_ref[...] = jnp.zeros_like(acc_ref)

    hb = h_ref[...]
    up = jnp.dot(hb, w1_ref[0].astype(BF16), preferred_element_type=F32)
    lin = jnp.dot(hb, w3_ref[0].astype(BF16), preferred_element_type=F32)
    hid = (_silu(up) * lin).astype(BF16)
    part = jnp.dot(hid, w2_ref[0].astype(BF16), preferred_element_type=F32)
    if moe:
        comb = comb_ref[...]
        lane = lax.broadcasted_iota(jnp.int32, comb.shape, 1)
        w = jnp.sum(jnp.where(lane == e, comb, 0.0), axis=-1, keepdims=True)
        part = part * w
    acc_ref[...] += part

    @pl.when((e == pl.num_programs(1) - 1) & (f == pl.num_programs(2) - 1))
    def _():
        y = x_ref[...] + _mod_row(mod_ref, row, 5) * acc_ref[...]
        if final:
            ms = jnp.mean(y * y, axis=-1, keepdims=True)
            y = y * lax.rsqrt(ms + EPS) * fg_ref[...]
        o_ref[...] = y


def _ffn(x2, comb, g, fg, mod, w1, w3, w2, *, tm, tf, tiles_per_batch, mod_row, final):
    t, d = x2.shape
    ne, _, ff = w1.shape
    moe = comb is not None
    in_specs = [pl.BlockSpec((tm, d), lambda i, e, f: (i, 0))]
    args = [x2]
    if moe:
        in_specs.append(pl.BlockSpec((tm, LANES), lambda i, e, f: (i, 0)))
        args.append(comb)
    in_specs += [
        pl.BlockSpec((1, d), lambda i, e, f: (0, 0)),
        pl.BlockSpec((1, d), lambda i, e, f: (0, 0)),
        pl.BlockSpec((SUBLANES, N_MOD * d), lambda i, e, f: (0, 0)),
        pl.BlockSpec((1, d, tf), lambda i, e, f: (e, 0, f)),
        pl.BlockSpec((1, d, tf), lambda i, e, f: (e, 0, f)),
        pl.BlockSpec((1, tf, d), lambda i, e, f: (e, f, 0)),
    ]
    args += [g.reshape(1, d), fg.reshape(1, d), mod, w1, w3, w2]
    return pl.pallas_call(
        functools.partial(_ffn_kernel, moe=moe, final=final,
                          tiles_per_batch=tiles_per_batch, mod_row=mod_row),
        grid=(t // tm, ne, ff // tf),
        in_specs=in_specs,
        out_specs=pl.BlockSpec((tm, d), lambda i, e, f: (i, 0)),
        out_shape=jax.ShapeDtypeStruct((t, d), F32),
        scratch_shapes=[pltpu.VMEM((tm, d), BF16), pltpu.VMEM((tm, d), F32)],
        compiler_params=pltpu.CompilerParams(
            dimension_semantics=("arbitrary", "arbitrary", "arbitrary"),
            vmem_limit_bytes=VMEM_LIMIT),
        name="moe_ffn" if moe else "dense_ffn",
    )(*args)


def _sincos_2d(rows, cols, dim):
    quarter = dim // 4
    omega = 1.0 / (10000.0 ** (jnp.arange(quarter, dtype=F32) / quarter))

    def emb1d(n):
        ang = jnp.arange(n, dtype=F32)[:, None] * omega[None, :]
        return jnp.concatenate([jnp.sin(ang), jnp.cos(ang)], axis=-1)

    er = jnp.broadcast_to(emb1d(rows)[:, None, :], (rows, cols, dim // 2))
    ec = jnp.broadcast_to(emb1d(cols)[None, :, :], (rows, cols, dim // 2))
    return jnp.concatenate([er, ec], axis=-1).reshape(rows * cols, dim)


def _block_diag_tiles(w):
    per = MXU_DIM // LRU_HEAD_DIM
    w4 = w.reshape(D_LRU // MXU_DIM, per, LRU_HEAD_DIM, LRU_HEAD_DIM)
    eye = jnp.eye(per, dtype=w.dtype)
    return jnp.einsum("ghij,hk->ghikj", w4, eye).reshape(D_LRU // MXU_DIM, MXU_DIM, MXU_DIM)


def _gate_weights(wa, wx):
    return jnp.concatenate([_block_diag_tiles(wa), _block_diag_tiles(wx)], axis=-1).astype(BF16)


def _lru_vectors(conv_w, conv_b, ba, bx, lam):
    rows = jnp.concatenate([conv_w, conv_b[None], ba[None], bx[None], lam[None]], axis=0)
    return jnp.pad(rows, ((0, 2 * SUBLANES - rows.shape[0]), (0, 0)))


def kernel(x, c, ctx, c_ctx, w_mod, b_mod, norm1_g, norm2_g, w_in, w_out, lru_conv_w, lru_conv_b, lru_wa, lru_ba, lru_wx, lru_bx, lru_lambda, conf_dw_w, conf_dw_b, conf_ln_g, conf_ln_b, ffn_w1, ffn_w3, ffn_w2, moe_router, moe_w1, moe_w3, moe_w2, final_g):
    nb, s, d = x.shape
    n_ctx = ctx.shape[1]
    pos = _sincos_2d(s // GRID_W, GRID_W, d)
    cc = jnp.concatenate([c, c_ctx[None], jnp.zeros((SUBLANES - nb - 1, d), F32)], axis=0)
    mod_all = _modulation(cc, w_mod, b_mod)

    ts_x, ts_c = 512, n_ctx
    tm_x = 1024
    tiles_per_batch = s // tm_x
    zero_state = jnp.zeros((nb, SUBLANES, D_LRU), F32)
    xc = ctx.reshape(1, nb * n_ctx, d)

    for l in range(DEPTH):
        last = l == DEPTH - 1
        mod = mod_all[l]
        w_in_b = w_in[l].astype(BF16)
        w_out_b = w_out[l].astype(BF16)
        pvec = [_lru_vectors(lru_conv_w[l], lru_conv_b[l], lru_ba[l, dr], lru_bx[l, dr],
                             lru_lambda[l, dr]) for dr in range(2)]
        wg = [_gate_weights(lru_wa[l, dr], lru_wx[l, dr]) for dr in range(2)]
        dww = jnp.pad(conf_dw_w[l], ((0, 4 * SUBLANES - CONF_CONV_W), (0, 0)))
        cvec = jnp.pad(jnp.stack([conf_dw_b[l], conf_ln_g[l], conf_ln_b[l]]),
                       ((0, SUBLANES - 3), (0, 0)))

        u_x, x = _in_proj(x, pos if l == 0 else None, norm1_g[l], mod, w_in_b,
                          tm=tm_x, mod_row=None)
        u_c, _ = _in_proj(xc, None, norm1_g[l], mod, w_in_b[:, :D_LRU] if last else w_in_b,
                          tm=512, mod_row=CTX_MOD_ROW)
        u_c = u_c.reshape(nb, n_ctx, u_c.shape[-1])
        hf_c, st_f = _lru_forward(u_c, pvec[0], wg[0], zero_state, ts=ts_c)
        hf_x, _ = _lru_forward(u_x, pvec[0], wg[0], st_f, ts=ts_x)
        if last:
            _, st_b = _lru_backward(u_c, pvec[1], wg[1], zero_state, ts=ts_c)
        else:
            xc_new, st_b = _lru_backward(
                u_c, pvec[1], wg[1], zero_state, ts=ts_c,
                merge_args=(hf_c, xc.reshape(nb, n_ctx, d), mod, dww, cvec, w_out_b),
                mod_row=CTX_MOD_ROW)
            xc = xc_new.reshape(1, nb * n_ctx, d)
        x, _ = _lru_backward(u_x, pvec[1], wg[1], st_b, ts=ts_x,
                             merge_args=(hf_x, x, mod, dww, cvec, w_out_b), mod_row=None)

        j = l // 2
        x2 = x.reshape(nb * s, d)
        if l % 2 == 0:
            w1, w3, w2 = ffn_w1[j][None], ffn_w3[j][None], ffn_w2[j][None]
            comb_x = comb_c = None
        else:
            w1, w3, w2 = moe_w1[j], moe_w3[j], moe_w2[j]
            wr = jnp.pad(moe_router[j], ((0, 0), (0, LANES - N_EXPERTS)))
            comb_x = _router(x2, norm2_g[l], mod, wr, tm=tm_x,
                             tiles_per_batch=tiles_per_batch, mod_row=None)
            comb_c = None if last else _router(xc[0], norm2_g[l], mod, wr, tm=512,
                                               tiles_per_batch=1, mod_row=CTX_MOD_ROW)
        x2 = _ffn(x2, comb_x, norm2_g[l], final_g, mod, w1, w3, w2, tm=tm_x, tf=256,
                  tiles_per_batch=tiles_per_batch, mod_row=None, final=last)
        x = x2.reshape(nb, s, d)
        if not last:
            xc2 = _ffn(xc[0], comb_c, norm2_g[l], final_g, mod, w1, w3, w2, tm=512, tf=256,
                       tiles_per_batch=1, mod_row=CTX_MOD_ROW, final=False)
            xc = xc2.reshape(1, nb * n_ctx, d)
    return x
```

```python
import functools
import math

import jax
import jax.numpy as jnp
from jax import lax
from jax.experimental import pallas as pl
from jax.experimental.pallas import tpu as pltpu

F32 = jnp.float32
BF16 = jnp.bfloat16

D_MODEL = 1024
DEPTH = 4
GRID_W = 64
D_LRU = 512
D_CONV = 512
LRU_HEADS = 8
LRU_HEAD_DIM = 64
LRU_CONV_W = 4
LRU_C = 8.0
CONF_CONV_W = 31
D_FF = 2816
N_EXPERTS = 8
N_MOD = 6
EPS = 1e-6

LANES = 128
SUBLANES = 8
BF16_ROWS = 16
MXU_DIM = 256
VMEM_LIMIT = 56 * 1024 * 1024

HALO = BF16_ROWS
CONV_CHUNK = 32
CTX_MOD_ROW = 4


def _silu(v):
    return v * jax.nn.sigmoid(v)


def _rms_mod(x, g, shift, scale):
    ms = jnp.mean(x * x, axis=-1, keepdims=True)
    y = x * lax.rsqrt(ms + EPS) * g
    return y * (1.0 + scale) + shift


def _mod_row(mod_ref, row, k):
    r = mod_ref[pl.ds(row, 1), :]
    return r[:, k * D_MODEL:(k + 1) * D_MODEL]


def _mod_kernel(cc_ref, w_ref, b_ref, o_ref):
    s = _silu(cc_ref[...])
    o_ref[0] = jnp.dot(s.astype(BF16), w_ref[0].astype(BF16),
                       preferred_element_type=F32) + b_ref[0]


def _modulation(cc, w_mod, b_mod):
    n_out = N_MOD * D_MODEL
    tn = 1536
    return pl.pallas_call(
        _mod_kernel,
        grid=(DEPTH, n_out // tn),
        in_specs=[
            pl.BlockSpec((SUBLANES, D_MODEL), lambda l, j: (0, 0)),
            pl.BlockSpec((1, D_MODEL, tn), lambda l, j: (l, 0, j)),
            pl.BlockSpec((1, 1, tn), lambda l, j: (l, 0, j)),
        ],
        out_specs=pl.BlockSpec((1, SUBLANES, tn), lambda l, j: (l, 0, j)),
        out_shape=jax.ShapeDtypeStruct((DEPTH, SUBLANES, n_out), F32),
        compiler_params=pltpu.CompilerParams(
            dimension_semantics=("arbitrary", "arbitrary"),
            vmem_limit_bytes=VMEM_LIMIT),
        name="modulation",
    )(cc, w_mod, b_mod.reshape(DEPTH, 1, n_out))


def _in_kernel(*refs, add_pos, mod_row):
    if add_pos:
        x_ref, pos_ref, g_ref, mod_ref, w_ref, u_ref, xo_ref = refs
    else:
        x_ref, g_ref, mod_ref, w_ref, u_ref = refs
    row = pl.program_id(0) if mod_row is None else mod_row
    x = x_ref[0]
    if add_pos:
        x = x + pos_ref[...]
        xo_ref[0] = x
    h = _rms_mod(x, g_ref[...], _mod_row(mod_ref, row, 0), _mod_row(mod_ref, row, 1))
    u_ref[0] = jnp.dot(h.astype(BF16), w_ref[...],
                       preferred_element_type=F32).astype(BF16)


def _in_proj(x, pos, g, mod, w, *, tm, mod_row):
    nb, s, d = x.shape
    n = w.shape[1]
    add_pos = pos is not None
    in_specs = [pl.BlockSpec((1, tm, d), lambda b, t: (b, t, 0))]
    args = [x]
    if add_pos:
        in_specs.append(pl.BlockSpec((tm, d), lambda b, t: (t, 0)))
        args.append(pos)
    in_specs += [
        pl.BlockSpec((1, d), lambda b, t: (0, 0)),
        pl.BlockSpec((SUBLANES, N_MOD * d), lambda b, t: (0, 0)),
        pl.BlockSpec((d, n), lambda b, t: (0, 0)),
    ]
    args += [g.reshape(1, d), mod, w]
    out_specs = [pl.BlockSpec((1, tm, n), lambda b, t: (b, t, 0))]
    out_shape = [jax.ShapeDtypeStruct((nb, s, n), BF16)]
    if add_pos:
        out_specs.append(pl.BlockSpec((1, tm, d), lambda b, t: (b, t, 0)))
        out_shape.append(jax.ShapeDtypeStruct((nb, s, d), F32))
    res = pl.pallas_call(
        functools.partial(_in_kernel, add_pos=add_pos, mod_row=mod_row),
        grid=(nb, s // tm),
        in_specs=in_specs,
        out_specs=out_specs,
        out_shape=out_shape,
        compiler_params=pltpu.CompilerParams(
            dimension_semantics=("arbitrary", "arbitrary"),
            vmem_limit_bytes=VMEM_LIMIT),
        name="in_proj",
    )(*args)
    return (res[0], res[1]) if add_pos else (res[0], x)


def _lru_coeffs(ext_ref, u_ref, prev_ref, next_ref, pvec_ref, wg_ref, has_prev, has_next, ts):
    prev = prev_ref[0].astype(F32)
    nxt = next_ref[0].astype(F32)
    ext_ref[0:SUBLANES, :] = prev[HALO - SUBLANES:HALO, :] * has_prev
    ext_ref[SUBLANES:SUBLANES + ts, :] = u_ref[0].astype(F32)
    ext_ref[SUBLANES + ts:2 * SUBLANES + ts, :] = nxt[0:SUBLANES, :] * has_next
    rc = pvec_ref[4:5, :]
    for k in range(LRU_CONV_W):
        rc = rc + pvec_ref[k:k + 1, :] * ext_ref[pl.ds(SUBLANES - 2 + k, ts), :]
    rcb = rc.astype(BF16)
    ga, gx = [], []
    for j in range(D_LRU // MXU_DIM):
        o = jnp.dot(rcb[:, j * MXU_DIM:(j + 1) * MXU_DIM], wg_ref[j],
                    preferred_element_type=F32)
        ga.append(o[:, :MXU_DIM])
        gx.append(o[:, MXU_DIM:])
    r_gate = jax.nn.sigmoid(jnp.concatenate(ga, axis=-1) + pvec_ref[5:6, :])
    i_gate = jax.nn.sigmoid(jnp.concatenate(gx, axis=-1) + pvec_ref[6:7, :])
    neg_lam = -pvec_ref[7:8, :]
    softplus = jnp.maximum(neg_lam, 0.0) + jnp.log1p(jnp.exp(-jnp.abs(neg_lam)))
    log_a = -LRU_C * r_gate * softplus
    a = jnp.exp(log_a)
    one_minus_a2 = -jnp.tanh(log_a) * (a * a + 1.0)
    b = jnp.sqrt(one_minus_a2) * (i_gate * rc)
    return a, b


def _lru_scan(a, b, h_in, hs_ref, ts, reverse):
    groups = ts // SUBLANES
    c = a.shape[-1]
    a3 = a.reshape(groups, SUBLANES, c)
    b3 = b.reshape(groups, SUBLANES, c)
    row = lax.broadcasted_iota(jnp.int32, (groups, SUBLANES, c), 1)
    for s in (1, 2, 4):
        shift = SUBLANES - s if reverse else s
        a_sh = pltpu.roll(a3, shift, axis=1)
        b_sh = pltpu.roll(b3, shift, axis=1)
        m = (row < SUBLANES - s) if reverse else (row >= s)
        b3 = jnp.where(m, a3 * b_sh + b3, b3)
        a3 = jnp.where(m, a3 * a_sh, a3)
    h = h_in
    order = range(groups - 1, -1, -1) if reverse else range(groups)
    edge = 0 if reverse else SUBLANES - 1
    for g in order:
        hg = a3[g] * h + b3[g]
        hs_ref[g * SUBLANES:(g + 1) * SUBLANES, :] = hg
        h = jnp.broadcast_to(hg[edge:edge + 1, :], (SUBLANES, c))
    return h


def _halo_specs(width, col, ts, s, tile_of):
    per = ts // HALO
    last = s // HALO - 1
    main = pl.BlockSpec((1, ts, width), lambda b, t: (b, tile_of(t), col))
    prev = pl.BlockSpec((1, HALO, width),
                        lambda b, t: (b, jnp.maximum(tile_of(t) * per - 1, 0), col))
    nxt = pl.BlockSpec((1, HALO, width),
                       lambda b, t: (b, jnp.minimum((tile_of(t) + 1) * per, last), col))
    return [main, prev, nxt]


def _fwd_kernel(u_ref, prev_ref, next_ref, pvec_ref, wg_ref, seed_ref,
                hf_ref, state_ref, ext_ref, hs_ref, carry_ref, *, ts):
    t = pl.program_id(1)
    nt = pl.num_programs(1)

    @pl.when(t == 0)
    def _():
        carry_ref[...] = seed_ref[0]

    has_prev = (t > 0).astype(F32)
    has_next = (t < nt - 1).astype(F32)
    a, b = _lru_coeffs(ext_ref, u_ref, prev_ref, next_ref, pvec_ref, wg_ref,
                       has_prev, has_next, ts)
    h = _lru_scan(a, b, carry_ref[...], hs_ref, ts, reverse=False)
    carry_ref[...] = h
    state_ref[0] = h
    hf_ref[0] = hs_ref[...].astype(BF16)


def _lru_forward(u, pvec, wg, seed, *, ts):
    nb, s, _ = u.shape
    return pl.pallas_call(
        functools.partial(_fwd_kernel, ts=ts),
        grid=(nb, s // ts),
        in_specs=_halo_specs(D_LRU, 0, ts, s, lambda t: t) + [
            pl.BlockSpec((2 * SUBLANES, D_LRU), lambda b, t: (0, 0)),
            pl.BlockSpec((D_LRU // MXU_DIM, MXU_DIM, 2 * MXU_DIM), lambda b, t: (0, 0, 0)),
            pl.BlockSpec((1, SUBLANES, D_LRU), lambda b, t: (b, 0, 0)),
        ],
        out_specs=[
            pl.BlockSpec((1, ts, D_LRU), lambda b, t: (b, t, 0)),
            pl.BlockSpec((1, SUBLANES, D_LRU), lambda b, t: (b, 0, 0)),
        ],
        out_shape=[
            jax.ShapeDtypeStruct((nb, s, D_LRU), BF16),
            jax.ShapeDtypeStruct((nb, SUBLANES, D_LRU), F32),
        ],
        scratch_shapes=[
            pltpu.VMEM((ts + 2 * SUBLANES, D_LRU), F32),
            pltpu.VMEM((ts, D_LRU), F32),
            pltpu.VMEM((SUBLANES, D_LRU), F32),
        ],
        compiler_params=pltpu.CompilerParams(
            dimension_semantics=("arbitrary", "arbitrary"),
            vmem_limit_bytes=VMEM_LIMIT),
        name="lru_forward",
    )(u, u, u, pvec, wg, seed)


def _bwd_kernel(*refs, ts, merge, mod_row):
    if merge:
        (u_ref, prev_ref, next_ref, pvec_ref, wg_ref, seed_ref,
         gate_ref, v_ref, vprev_ref, vnext_ref, hf_ref, x_ref, mod_ref,
         dww_ref, cvec_ref, wo_ref,
         xo_ref, state_ref,
         ext_ref, hs_ref, carry_ref, g_ref, cs_ref, sh_ref) = refs
    else:
        (u_ref, prev_ref, next_ref, pvec_ref, wg_ref, seed_ref,
         state_ref, ext_ref, hs_ref, carry_ref) = refs
    tg = pl.program_id(1)
    nt = pl.num_programs(1)
    t = nt - 1 - tg

    @pl.when(tg == 0)
    def _():
        carry_ref[...] = seed_ref[0]

    has_prev = (t > 0).astype(F32)
    has_next = (t < nt - 1).astype(F32)
    a, b = _lru_coeffs(ext_ref, u_ref, prev_ref, next_ref, pvec_ref, wg_ref,
                       has_prev, has_next, ts)
    h = _lru_scan(a, b, carry_ref[...], hs_ref, ts, reverse=True)
    carry_ref[...] = h
    state_ref[0] = h
    if not merge:
        return

    y = hf_ref[0].astype(F32) + hs_ref[...]
    gate = gate_ref[0].astype(F32)
    cdf = 0.5 * (1.0 + jnp.tanh(math.sqrt(2.0 / math.pi) * (gate + 0.044715 * (gate * gate * gate))))
    lru = (y * (gate * cdf)).astype(BF16)

    def glu(vref):
        v = vref[0].astype(F32)
        return v[:, :D_CONV] * jax.nn.sigmoid(v[:, D_CONV:])

    g_ref[0:HALO, :] = glu(vprev_ref) * has_prev
    g_ref[HALO:HALO + ts, :] = glu(v_ref)
    g_ref[HALO + ts:2 * HALO + ts, :] = glu(vnext_ref) * has_next
    pad = (CONF_CONV_W - 1) // 2
    dwb = cvec_ref[0:1, :]
    span = ts + 2 * HALO - SUBLANES
    for r in range(1, SUBLANES):
        sh_ref[r - 1, 0:span, :] = g_ref[pl.ds(r, span), :]

    def conv_chunk(i, carry):
        base = pl.multiple_of(i * CONV_CHUNK, CONV_CHUNK)
        acc = jnp.zeros((CONV_CHUNK, D_CONV), F32) + dwb
        for k in range(CONF_CONV_W):
            q, r = divmod(HALO - pad + k, SUBLANES)
            rows = pl.ds(base + q * SUBLANES, CONV_CHUNK)
            tap = g_ref[rows, :] if r == 0 else sh_ref[r - 1, rows, :]
            acc = acc + dww_ref[k:k + 1, :] * tap
        cs_ref[pl.ds(base, CONV_CHUNK), :] = acc
        return carry

    lax.fori_loop(0, ts // CONV_CHUNK, conv_chunk, 0)
    cv = cs_ref[...]
    mu = jnp.mean(cv, axis=-1, keepdims=True)
    var = jnp.mean(jnp.square(cv - mu), axis=-1, keepdims=True)
    ln = (cv - mu) * lax.rsqrt(var + EPS) * cvec_ref[1:2, :] + cvec_ref[2:3, :]
    conf = _silu(ln).astype(BF16)

    o = jnp.dot(lru, wo_ref[0:D_LRU, :], preferred_element_type=F32)
    o = o + jnp.dot(conf, wo_ref[D_LRU:, :], preferred_element_type=F32)
    row = pl.program_id(0) if mod_row is None else mod_row
    xo_ref[0] = x_ref[0] + _mod_row(mod_ref, row, 2) * o


def _lru_backward(u, pvec, wg, seed, *, ts, merge_args=None, mod_row=None):
    nb, s, _ = u.shape
    nt = s // ts
    rev = lambda t: nt - 1 - t
    in_specs = _halo_specs(D_LRU, 0, ts, s, rev) + [
        pl.BlockSpec((2 * SUBLANES, D_LRU), lambda b, t: (0, 0)),
        pl.BlockSpec((D_LRU // MXU_DIM, MXU_DIM, 2 * MXU_DIM), lambda b, t: (0, 0, 0)),
        pl.BlockSpec((1, SUBLANES, D_LRU), lambda b, t: (b, 0, 0)),
    ]
    args = [u, u, u, pvec, wg, seed]
    state_spec = pl.BlockSpec((1, SUBLANES, D_LRU), lambda b, t: (b, 0, 0))
    state_shape = jax.ShapeDtypeStruct((nb, SUBLANES, D_LRU), F32)
    scratch = [
        pltpu.VMEM((ts + 2 * SUBLANES, D_LRU), F32),
        pltpu.VMEM((ts, D_LRU), F32),
        pltpu.VMEM((SUBLANES, D_LRU), F32),
    ]
    merge = merge_args is not None
    if merge:
        hf, x, mod, dww, cvec, wo = merge_args
        d = x.shape[-1]
        in_specs += [pl.BlockSpec((1, ts, D_LRU), lambda b, t: (b, rev(t), 1))]
        in_specs += _halo_specs(2 * D_CONV, 1, ts, s, rev)
        in_specs += [
            pl.BlockSpec((1, ts, D_LRU), lambda b, t: (b, rev(t), 0)),
            pl.BlockSpec((1, ts, d), lambda b, t: (b, rev(t), 0)),
            pl.BlockSpec((SUBLANES, N_MOD * d), lambda b, t: (0, 0)),
            pl.BlockSpec((4 * SUBLANES, D_CONV), lambda b, t: (0, 0)),
            pl.BlockSpec((SUBLANES, D_CONV), lambda b, t: (0, 0)),
            pl.BlockSpec((d, d), lambda b, t: (0, 0)),
        ]
        args += [u, u, u, u, hf, x, mod, dww, cvec, wo]
        out_specs = [pl.BlockSpec((1, ts, d), lambda b, t: (b, rev(t), 0)), state_spec]
        out_shape = [jax.ShapeDtypeStruct((nb, s, d), F32), state_shape]
        scratch += [
            pltpu.VMEM((ts + 2 * HALO, D_CONV), F32),
            pltpu.VMEM((ts, D_CONV), F32),
            pltpu.VMEM((SUBLANES - 1, ts + 2 * HALO, D_CONV), F32),
        ]
    else:
        out_specs = [state_spec]
        out_shape = [state_shape]
    res = pl.pallas_call(
        functools.partial(_bwd_kernel, ts=ts, merge=merge, mod_row=mod_row),
        grid=(nb, nt),
        in_specs=in_specs,
        out_specs=out_specs,
        out_shape=out_shape,
        scratch_shapes=scratch,
        compiler_params=pltpu.CompilerParams(
            dimension_semantics=("arbitrary", "arbitrary"),
            vmem_limit_bytes=VMEM_LIMIT),
        name="lru_backward_merge" if merge else "lru_backward",
    )(*args)
    return res if merge else (None, res[0])


ROUTE_BLOCK = 512
CHUNK = BF16_ROWS
BLOCK_CAP = 2 * ROUTE_BLOCK + LANES
CHUNKS_PER_BLOCK = BLOCK_CAP // CHUNK
MOE_TILE = 1024
CHUNKS_PER_TILE = MOE_TILE // CHUNK


def _route_kernel(x_ref, g_ref, mod_ref, wr_ref, hs_ref, ws_ref, cnt_ref, pos_ref,
                  *, blocks_per_batch, n_lat_blocks):
    i = pl.program_id(0)
    row = jnp.where(i < n_lat_blocks, i // blocks_per_batch, CTX_MOD_ROW)
    h = _rms_mod(x_ref[...], g_ref[...], _mod_row(mod_ref, row, 3), _mod_row(mod_ref, row, 4))
    logits = jnp.dot(h, wr_ref[...], preferred_element_type=F32,
                     precision=lax.Precision.HIGHEST)
    lane = lax.broadcasted_iota(jnp.int32, logits.shape, 1).astype(F32)
    neg = jnp.float32(-jnp.inf)
    l1 = jnp.where(lane < N_EXPERTS, logits, neg)
    m1 = jnp.max(l1, axis=-1, keepdims=True)
    i1 = jnp.min(jnp.where(l1 == m1, lane, float(LANES)), axis=-1, keepdims=True)
    l2 = jnp.where(lane == i1, neg, l1)
    m2 = jnp.max(l2, axis=-1, keepdims=True)
    i2 = jnp.min(jnp.where(l2 == m2, lane, float(LANES)), axis=-1, keepdims=True)
    e = jnp.exp(m2 - m1)
    p1 = 1.0 / (1.0 + e)
    p2 = e / (1.0 + e)

    sel1 = lane == i1
    sel2 = lane == i2
    member = jnp.where(sel1 | sel2, 1.0, 0.0)
    cnt = jnp.sum(member, axis=0, keepdims=True)
    padded = jnp.floor((cnt + (CHUNK - 1.0)) * (1.0 / CHUNK)) * CHUNK
    r128 = lax.broadcasted_iota(jnp.int32, (LANES, LANES), 0)
    c128 = lax.broadcasted_iota(jnp.int32, (LANES, LANES), 1)
    before = jnp.where(r128 < c128, 1.0, 0.0)
    seg_off = jnp.dot(jnp.broadcast_to(padded, (SUBLANES, LANES)), before,
                      preferred_element_type=F32, precision=lax.Precision.HIGHEST)[0:1, :]
    bt = member.shape[0]
    rt = lax.broadcasted_iota(jnp.int32, (bt, bt), 0)
    ct = lax.broadcasted_iota(jnp.int32, (bt, bt), 1)
    earlier = jnp.where(ct < rt, 1.0, 0.0).astype(BF16)
    rank = jnp.dot(earlier, member.astype(BF16), preferred_element_type=F32)
    slot_of = seg_off + rank
    pos1 = jnp.sum(jnp.where(sel1, slot_of, 0.0), axis=-1, keepdims=True)
    pos2 = jnp.sum(jnp.where(sel2, slot_of, 0.0), axis=-1, keepdims=True)
    stats = (jnp.where(lane == 0.0, pos1, 0.0) + jnp.where(lane == 1.0, pos2, 0.0)
             + jnp.where(lane == 2.0, p1, 0.0) + jnp.where(lane == 3.0, p2, 0.0))
    pos_ref[...] = stats
    cnt_ref[0] = jnp.broadcast_to(cnt, (SUBLANES, LANES))

    st = stats.T
    slot = lax.broadcasted_iota(jnp.int32, (BLOCK_CAP, bt), 0).astype(F32)
    hit1 = slot == st[0:1, :]
    hit2 = slot == st[1:2, :]
    perm = jnp.where(hit1 | hit2, 1.0, 0.0).astype(BF16)
    hs_ref[...] = jnp.dot(perm, h.astype(BF16), preferred_element_type=F32).astype(BF16)
    wrow = jnp.where(hit1, st[2:3, :], 0.0) + jnp.where(hit2, st[3:4, :], 0.0)
    ws_ref[...] = jnp.broadcast_to(jnp.sum(wrow, axis=-1, keepdims=True), (BLOCK_CAP, LANES))


def _route(xall, g, mod, wr, *, n_lat_blocks, blocks_per_batch):
    t, d = xall.shape
    nblk = t // ROUTE_BLOCK
    return pl.pallas_call(
        functools.partial(_route_kernel, blocks_per_batch=blocks_per_batch,
                          n_lat_blocks=n_lat_blocks),
        grid=(nblk,),
        in_specs=[
            pl.BlockSpec((ROUTE_BLOCK, d), lambda i: (i, 0)),
            pl.BlockSpec((1, d), lambda i: (0, 0)),
            pl.BlockSpec((SUBLANES, N_MOD * d), lambda i: (0, 0)),
            pl.BlockSpec((d, LANES), lambda i: (0, 0)),
        ],
        out_specs=[
            pl.BlockSpec((BLOCK_CAP, d), lambda i: (i, 0)),
            pl.BlockSpec((BLOCK_CAP, LANES), lambda i: (i, 0)),
            pl.BlockSpec((1, SUBLANES, LANES), lambda i: (i, 0, 0)),
            pl.BlockSpec((ROUTE_BLOCK, LANES), lambda i: (i, 0)),
        ],
        out_shape=[
            jax.ShapeDtypeStruct((nblk * BLOCK_CAP, d), BF16),
            jax.ShapeDtypeStruct((nblk * BLOCK_CAP, LANES), F32),
            jax.ShapeDtypeStruct((nblk, SUBLANES, LANES), F32),
            jax.ShapeDtypeStruct((t, LANES), F32),
        ],
        compiler_params=pltpu.CompilerParams(
            dimension_semantics=("arbitrary",),
            vmem_limit_bytes=VMEM_LIMIT),
        name="route",
    )(xall, g.reshape(1, d), mod, wr)


def _route_tables(cnt, n_tiles_max):
    nblk = cnt.shape[0]
    n_be = (cnt[:, 0, :N_EXPERTS].astype(jnp.int32) + (CHUNK - 1)) // CHUNK
    ends_be = jnp.cumsum(n_be, axis=1)
    off_be = ends_be - n_be
    start_be = jnp.cumsum(n_be, axis=0) - n_be
    g_e = jnp.sum(n_be, axis=0)
    tiles_e = (g_e + (CHUNKS_PER_TILE - 1)) // CHUNKS_PER_TILE
    tend_e = jnp.cumsum(tiles_e)
    tstart_e = tend_e - tiles_e
    n_tiles = tend_e[-1]

    c = jnp.arange(CHUNKS_PER_BLOCK, dtype=jnp.int32)
    e_bc = jnp.sum((c[None, :, None] >= ends_be[:, None, :]).astype(jnp.int32), axis=-1)
    used = e_bc < N_EXPERTS
    e_cl = jnp.minimum(e_bc, N_EXPERTS - 1)
    j_bc = c[None, :] - jnp.take_along_axis(off_be, e_cl, axis=1)
    d_bc = (tstart_e[e_cl] * CHUNKS_PER_TILE + jnp.take_along_axis(start_be, e_cl, axis=1) + j_bc)
    dst = jnp.where(used, d_bc, -1).reshape(-1).astype(jnp.int32)
    n_slots = n_tiles_max * CHUNKS_PER_TILE
    src_ids = jnp.arange(nblk * CHUNKS_PER_BLOCK, dtype=jnp.int32)
    src = jnp.zeros((n_slots,), jnp.int32).at[jnp.where(dst >= 0, dst, n_slots)].set(src_ids, mode="drop")

    ti = jnp.arange(n_tiles_max, dtype=jnp.int32)
    ti_cl = jnp.minimum(ti, n_tiles - 1)
    tile_expert = jnp.sum((ti_cl[:, None] >= tend_e[None, :]).astype(jnp.int32), axis=-1)
    tile_expert = jnp.minimum(tile_expert, N_EXPERTS - 1)
    left = g_e[tile_expert] - (ti - tstart_e[tile_expert]) * CHUNKS_PER_TILE
    tile_nvalid = jnp.where(ti < n_tiles, jnp.clip(left, 0, CHUNKS_PER_TILE), 0).astype(jnp.int32)
    return (tile_expert.astype(jnp.int32), tile_nvalid, src,
            n_tiles.reshape(1).astype(jnp.int32), dst)


def _gather_chunks(table_ref, base, count, src_hbm, dst_buf, sem, n_slots):
    def copy(j):
        cid = table_ref[base + j]
        return pltpu.make_async_copy(
            src_hbm.at[pl.ds(pl.multiple_of(cid * CHUNK, CHUNK), CHUNK), :],
            dst_buf.at[pl.ds(pl.multiple_of(j * CHUNK, CHUNK), CHUNK), :], sem)

    def start(j, carry):
        @pl.when((j < count) & (table_ref[base + j] >= 0))
        def _():
            copy(j).start()
        return carry

    def wait(j, carry):
        @pl.when((j < count) & (table_ref[base + j] >= 0))
        def _():
            copy(j).wait()
        return carry

    lax.fori_loop(0, n_slots, start, 0)
    return functools.partial(lax.fori_loop, 0, n_slots, wait, 0)


def _moe_kernel(te_ref, nv_ref, src_ref, nt_ref, hs_hbm, ws_hbm, w1_ref, w3_ref, w2_ref,
                ys_ref, xbuf, wbuf, acc_ref, sems):
    i = pl.program_id(0)
    f = pl.program_id(1)
    valid = i < nt_ref[0]

    @pl.when((i == 0) & (f == 0))
    def _():
        xbuf[...] = jnp.zeros_like(xbuf)
        wbuf[...] = jnp.zeros_like(wbuf)

    @pl.when(valid & (f == 0))
    def _():
        base = i * CHUNKS_PER_TILE
        wait_x = _gather_chunks(src_ref, base, nv_ref[i], hs_hbm, xbuf, sems.at[0], CHUNKS_PER_TILE)
        wait_w = _gather_chunks(src_ref, base, nv_ref[i], ws_hbm, wbuf, sems.at[1], CHUNKS_PER_TILE)
        acc_ref[...] = jnp.zeros_like(acc_ref)
        wait_x()
        wait_w()

    @pl.when(valid)
    def _():
        xb = xbuf[...]
        up = jnp.dot(xb, w1_ref[0, 0].astype(BF16), preferred_element_type=F32)
        lin = jnp.dot(xb, w3_ref[0, 0].astype(BF16), preferred_element_type=F32)
        hid = (_silu(up) * lin).astype(BF16)
        acc_ref[...] += jnp.dot(hid, w2_ref[0, 0].astype(BF16), preferred_element_type=F32)

    @pl.when(valid & (f == pl.num_programs(1) - 1))
    def _():
        ys_ref[...] = (acc_ref[...] * wbuf[:, 0:1]).astype(BF16)

    @pl.when(jnp.logical_not(valid) & (f == 0))
    def _():
        ys_ref[...] = jnp.zeros_like(ys_ref)


def _moe_grouped(hs, ws, tables, w1, w3, w2, layer, *, n_tiles_max, tf):
    tile_expert, tile_nvalid, src, n_tiles, _ = tables
    d = hs.shape[1]
    ff = w1.shape[-1]
    nf = ff // tf

    def w_in_map(i, f, te, nv, sr, nt):
        return (layer, te[i], 0, jnp.where(i < nt[0], f, nf - 1))

    def w_out_map(i, f, te, nv, sr, nt):
        return (layer, te[i], jnp.where(i < nt[0], f, nf - 1), 0)

    def y_map(i, f, te, nv, sr, nt):
        return (i, 0)

    grid_spec = pltpu.PrefetchScalarGridSpec(
        num_scalar_prefetch=4,
        grid=(n_tiles_max, nf),
        in_specs=[
            pl.BlockSpec(memory_space=pl.ANY),
            pl.BlockSpec(memory_space=pl.ANY),
            pl.BlockSpec((1, 1, d, tf), w_in_map),
            pl.BlockSpec((1, 1, d, tf), w_in_map),
            pl.BlockSpec((1, 1, tf, d), w_out_map),
        ],
        out_specs=pl.BlockSpec((MOE_TILE, d), y_map),
        scratch_shapes=[
            pltpu.VMEM((MOE_TILE, d), BF16),
            pltpu.VMEM((MOE_TILE, LANES), F32),
            pltpu.VMEM((MOE_TILE, d), F32),
            pltpu.SemaphoreType.DMA((2,)),
        ],
    )
    return pl.pallas_call(
        _moe_kernel,
        grid_spec=grid_spec,
        out_shape=jax.ShapeDtypeStruct((n_tiles_max * MOE_TILE, d), BF16),
        compiler_params=pltpu.CompilerParams(
            dimension_semantics=("arbitrary", "arbitrary"),
            vmem_limit_bytes=VMEM_LIMIT),
        name="moe_grouped",
    )(tile_expert, tile_nvalid, src, n_tiles, hs, ws, w1, w3, w2)


def _combine_kernel(dst_ref, ys_hbm, pos_ref, x_ref, fg_ref, mod_ref, o_ref, ybuf, sem,
                    *, blocks_per_batch, n_lat_blocks, final):
    i = pl.program_id(0)
    row = jnp.where(i < n_lat_blocks, i // blocks_per_batch, CTX_MOD_ROW)

    @pl.when(i == 0)
    def _():
        ybuf[...] = jnp.zeros_like(ybuf)

    wait = _gather_chunks(dst_ref, i * CHUNKS_PER_BLOCK, CHUNKS_PER_BLOCK, ys_hbm, ybuf, sem.at[0],
                          CHUNKS_PER_BLOCK)
    stats = pos_ref[...]
    slot = lax.broadcasted_iota(jnp.int32, (stats.shape[0], BLOCK_CAP), 1).astype(F32)
    pick = jnp.where((slot == stats[:, 0:1]) | (slot == stats[:, 1:2]), 1.0, 0.0).astype(BF16)
    wait()
    moe = jnp.dot(pick, ybuf[...], preferred_element_type=F32)
    y = x_ref[...] + _mod_row(mod_ref, row, 5) * moe
    if final:
        ms = jnp.mean(y * y, axis=-1, keepdims=True)
        y = y * lax.rsqrt(ms + EPS) * fg_ref[...]
    o_ref[...] = y


def _combine(ys, dst, pos, xall, fg, mod, *, n_lat_blocks, blocks_per_batch, final):
    t, d = xall.shape
    nblk = t // ROUTE_BLOCK
    grid_spec = pltpu.PrefetchScalarGridSpec(
        num_scalar_prefetch=1,
        grid=(nblk,),
        in_specs=[
            pl.BlockSpec(memory_space=pl.ANY),
            pl.BlockSpec((ROUTE_BLOCK, LANES), lambda i, ds: (i, 0)),
            pl.BlockSpec((ROUTE_BLOCK, d), lambda i, ds: (i, 0)),
            pl.BlockSpec((1, d), lambda i, ds: (0, 0)),
            pl.BlockSpec((SUBLANES, N_MOD * d), lambda i, ds: (0, 0)),
        ],
        out_specs=pl.BlockSpec((ROUTE_BLOCK, d), lambda i, ds: (i, 0)),
        scratch_shapes=[
            pltpu.VMEM((BLOCK_CAP, d), BF16),
            pltpu.SemaphoreType.DMA((1,)),
        ],
    )
    return pl.pallas_call(
        functools.partial(_combine_kernel, blocks_per_batch=blocks_per_batch,
                          n_lat_blocks=n_lat_blocks, final=final),
        grid_spec=grid_spec,
        out_shape=jax.ShapeDtypeStruct((t, d), F32),
        compiler_params=pltpu.CompilerParams(
            dimension_semantics=("arbitrary",),
            vmem_limit_bytes=VMEM_LIMIT),
        name="moe_combine",
    )(dst, ys, pos, xall, fg.reshape(1, d), mod)


def _ffn_kernel(x_ref, g_ref, mod_ref, w1_ref, w3_ref, w2_ref, o_ref, h_ref, acc_ref,
                *, tiles_per_batch, mod_row):
    i = pl.program_id(0)
    f = pl.program_id(1)
    row = i // tiles_per_batch if mod_row is None else mod_row

    @pl.when(f == 0)
    def _():
        h = _rms_mod(x_ref[...], g_ref[...], _mod_row(mod_ref, row, 3), _mod_row(mod_ref, row, 4))
        h_ref[...] = h.astype(BF16)
        acc_ref[...] = jnp.zeros_like(acc_ref)

    hb = h_ref[...]
    up = jnp.dot(hb, w1_ref[0].astype(BF16), preferred_element_type=F32)
    lin = jnp.dot(hb, w3_ref[0].astype(BF16), preferred_element_type=F32)
    hid = (_silu(up) * lin).astype(BF16)
    acc_ref[...] += jnp.dot(hid, w2_ref[0].astype(BF16), preferred_element_type=F32)

    @pl.when(f == pl.num_programs(1) - 1)
    def _():
        o_ref[...] = x_ref[...] + _mod_row(mod_ref, row, 5) * acc_ref[...]


def _ffn(x2, g, mod, w1, w3, w2, layer, *, tm, tf, tiles_per_batch, mod_row):
    t, d = x2.shape
    ff = w1.shape[-1]
    return pl.pallas_call(
        functools.partial(_ffn_kernel, tiles_per_batch=tiles_per_batch, mod_row=mod_row),
        grid=(t // tm, ff // tf),
        in_specs=[
            pl.BlockSpec((tm, d), lambda i, f: (i, 0)),
            pl.BlockSpec((1, d), lambda i, f: (0, 0)),
            pl.BlockSpec((SUBLANES, N_MOD * d), lambda i, f: (0, 0)),
            pl.BlockSpec((1, d, tf), lambda i, f: (layer, 0, f)),
            pl.BlockSpec((1, d, tf), lambda i, f: (layer, 0, f)),
            pl.BlockSpec((1, tf, d), lambda i, f: (layer, f, 0)),
        ],
        out_specs=pl.BlockSpec((tm, d), lambda i, f: (i, 0)),
        out_shape=jax.ShapeDtypeStruct((t, d), F32),
        scratch_shapes=[pltpu.VMEM((tm, d), BF16), pltpu.VMEM((tm, d), F32)],
        compiler_params=pltpu.CompilerParams(
            dimension_semantics=("arbitrary", "arbitrary"),
            vmem_limit_bytes=VMEM_LIMIT),
        name="dense_ffn",
    )(x2, g.reshape(1, d), mod, w1, w3, w2)


def _sincos_2d(rows, cols, dim):
    quarter = dim // 4
    omega = 1.0 / (10000.0 ** (jnp.arange(quarter, dtype=F32) / quarter))

    def emb1d(n):
        ang = jnp.arange(n, dtype=F32)[:, None] * omega[None, :]
        return jnp.concatenate([jnp.sin(ang), jnp.cos(ang)], axis=-1)

    er = jnp.broadcast_to(emb1d(rows)[:, None, :], (rows, cols, dim // 2))
    ec = jnp.broadcast_to(emb1d(cols)[None, :, :], (rows, cols, dim // 2))
    return jnp.concatenate([er, ec], axis=-1).reshape(rows * cols, dim)


def _block_diag_tiles(w):
    per = MXU_DIM // LRU_HEAD_DIM
    w4 = w.reshape(D_LRU // MXU_DIM, per, LRU_HEAD_DIM, LRU_HEAD_DIM)
    eye = jnp.eye(per, dtype=w.dtype)
    return jnp.einsum("ghij,hk->ghikj", w4, eye).reshape(D_LRU // MXU_DIM, MXU_DIM, MXU_DIM)


def _gate_weights(wa, wx):
    return jnp.concatenate([_block_diag_tiles(wa), _block_diag_tiles(wx)], axis=-1).astype(BF16)


def _lru_vectors(conv_w, conv_b, ba, bx, lam):
    rows = jnp.concatenate([conv_w, conv_b[None], ba[None], bx[None], lam[None]], axis=0)
    return jnp.pad(rows, ((0, 2 * SUBLANES - rows.shape[0]), (0, 0)))


def kernel(x, c, ctx, c_ctx, w_mod, b_mod, norm1_g, norm2_g, w_in, w_out, lru_conv_w, lru_conv_b, lru_wa, lru_ba, lru_wx, lru_bx, lru_lambda, conf_dw_w, conf_dw_b, conf_ln_g, conf_ln_b, ffn_w1, ffn_w3, ffn_w2, moe_router, moe_w1, moe_w3, moe_w2, final_g):
    nb, s, d = x.shape
    n_ctx = ctx.shape[1]
    pos = _sincos_2d(s // GRID_W, GRID_W, d)
    cc = jnp.concatenate([c, c_ctx[None], jnp.zeros((SUBLANES - nb - 1, d), F32)], axis=0)
    mod_all = _modulation(cc, w_mod, b_mod)

    ts_x, ts_c = 512, n_ctx
    tm_x = 1024
    tiles_per_batch = s // tm_x
    zero_state = jnp.zeros((nb, SUBLANES, D_LRU), F32)
    xc = ctx.reshape(1, nb * n_ctx, d)

    for l in range(DEPTH):
        last = l == DEPTH - 1
        mod = mod_all[l]
        w_in_b = w_in[l].astype(BF16)
        w_out_b = w_out[l].astype(BF16)
        pvec = [_lru_vectors(lru_conv_w[l], lru_conv_b[l], lru_ba[l, dr], lru_bx[l, dr],
                             lru_lambda[l, dr]) for dr in range(2)]
        wg = [_gate_weights(lru_wa[l, dr], lru_wx[l, dr]) for dr in range(2)]
        dww = jnp.pad(conf_dw_w[l], ((0, 4 * SUBLANES - CONF_CONV_W), (0, 0)))
        cvec = jnp.pad(jnp.stack([conf_dw_b[l], conf_ln_g[l], conf_ln_b[l]]),
                       ((0, SUBLANES - 3), (0, 0)))

        u_x, x = _in_proj(x, pos if l == 0 else None, norm1_g[l], mod, w_in_b,
                          tm=tm_x, mod_row=None)
        u_c, _ = _in_proj(xc, None, norm1_g[l], mod, w_in_b[:, :D_LRU] if last else w_in_b,
                          tm=512, mod_row=CTX_MOD_ROW)
        u_c = u_c.reshape(nb, n_ctx, u_c.shape[-1])
        hf_c, st_f = _lru_forward(u_c, pvec[0], wg[0], zero_state, ts=ts_c)
        hf_x, _ = _lru_forward(u_x, pvec[0], wg[0], st_f, ts=ts_x)
        if last:
            _, st_b = _lru_backward(u_c, pvec[1], wg[1], zero_state, ts=ts_c)
        else:
            xc_new, st_b = _lru_backward(
                u_c, pvec[1], wg[1], zero_state, ts=ts_c,
                merge_args=(hf_c, xc.reshape(nb, n_ctx, d), mod, dww, cvec, w_out_b),
                mod_row=CTX_MOD_ROW)
            xc = xc_new.reshape(1, nb * n_ctx, d)
        x, _ = _lru_backward(u_x, pvec[1], wg[1], st_b, ts=ts_x,
                             merge_args=(hf_x, x, mod, dww, cvec, w_out_b), mod_row=None)

        j = l // 2
        x2 = x.reshape(nb * s, d)
        if l % 2 == 0:
            x2 = _ffn(x2, norm2_g[l], mod, ffn_w1, ffn_w3, ffn_w2, j, tm=tm_x, tf=256,
                      tiles_per_batch=tiles_per_batch, mod_row=None)
            xc2 = _ffn(xc[0], norm2_g[l], mod, ffn_w1, ffn_w3, ffn_w2, j, tm=512, tf=256,
                       tiles_per_batch=1, mod_row=CTX_MOD_ROW)
            xc = xc2.reshape(1, nb * n_ctx, d)
        else:
            xall = x2 if last else jnp.concatenate([x2, xc[0]], axis=0)
            nblk = xall.shape[0] // ROUTE_BLOCK
            n_lat_blocks = x2.shape[0] // ROUTE_BLOCK
            blocks_per_batch = s // ROUTE_BLOCK
            n_tiles_max = (nblk * CHUNKS_PER_BLOCK + N_EXPERTS * (CHUNKS_PER_TILE - 1)) // CHUNKS_PER_TILE
            wr = jnp.pad(moe_router[j], ((0, 0), (0, LANES - N_EXPERTS)))
            hs, ws, cnt, pos_tok = _route(xall, norm2_g[l], mod, wr, n_lat_blocks=n_lat_blocks,
                                          blocks_per_batch=blocks_per_batch)
            tables = _route_tables(cnt, n_tiles_max)
            ys = _moe_grouped(hs, ws, tables, moe_w1, moe_w3, moe_w2, j,
                              n_tiles_max=n_tiles_max, tf=256)
            xall = _combine(ys, tables[4], pos_tok, xall, final_g, mod, n_lat_blocks=n_lat_blocks,
                            blocks_per_batch=blocks_per_batch, final=last)
            x2 = xall[:x2.shape[0]]
            if not last:
                xc = xall[x2.shape[0]:].reshape(1, nb * n_ctx, d)
        x = x2.reshape(nb, s, d)
    return x
```

```python
import functools
import math

import jax
import jax.numpy as jnp
from jax import lax
from jax.experimental import pallas as pl
from jax.experimental.pallas import tpu as pltpu

F32 = jnp.float32
BF16 = jnp.bfloat16

D_MODEL = 1024
DEPTH = 4
GRID_W = 64
D_LRU = 512
D_CONV = 512
LRU_HEADS = 8
LRU_HEAD_DIM = 64
LRU_CONV_W = 4
LRU_C = 8.0
CONF_CONV_W = 31
D_FF = 2816
N_EXPERTS = 8
N_MOD = 6
EPS = 1e-6

LANES = 128
SUBLANES = 8
BF16_ROWS = 16
MXU_DIM = 256
VMEM_LIMIT = 60 * 1024 * 1024

HALO = BF16_ROWS
CONV_CHUNK = 32
CTX_MOD_ROW = 4


def _sigmoid(v):
    return 0.5 * jnp.tanh(0.5 * v) + 0.5


def _silu(v):
    h = 0.5 * v
    return h + h * jnp.tanh(h)


def _rms_mod(x, g, shift, scale):
    ms = jnp.mean(x * x, axis=-1, keepdims=True)
    y = x * lax.rsqrt(ms + EPS) * g
    return y * (1.0 + scale) + shift


def _mod_row(mod_ref, row, k):
    r = mod_ref[pl.ds(row, 1), :]
    return r[:, k * D_MODEL:(k + 1) * D_MODEL]


def _mod_kernel(cc_ref, w_ref, b_ref, o_ref):
    s = _silu(cc_ref[...])
    o_ref[0] = jnp.dot(s.astype(BF16), w_ref[0].astype(BF16),
                       preferred_element_type=F32) + b_ref[0]


def _modulation(cc, w_mod, b_mod):
    n_out = N_MOD * D_MODEL
    tn = 1536
    return pl.pallas_call(
        _mod_kernel,
        grid=(DEPTH, n_out // tn),
        in_specs=[
            pl.BlockSpec((SUBLANES, D_MODEL), lambda l, j: (0, 0)),
            pl.BlockSpec((1, D_MODEL, tn), lambda l, j: (l, 0, j)),
            pl.BlockSpec((1, 1, tn), lambda l, j: (l, 0, j)),
        ],
        out_specs=pl.BlockSpec((1, SUBLANES, tn), lambda l, j: (l, 0, j)),
        out_shape=jax.ShapeDtypeStruct((DEPTH, SUBLANES, n_out), F32),
        compiler_params=pltpu.CompilerParams(
            dimension_semantics=("arbitrary", "arbitrary"),
            vmem_limit_bytes=VMEM_LIMIT),
        name="modulation",
    )(cc, w_mod, b_mod.reshape(DEPTH, 1, n_out))


def _in_kernel(*refs, add_pos, mod_row):
    if add_pos:
        x_ref, pos_ref, g_ref, mod_ref, w_ref, u_ref, xo_ref = refs
    else:
        x_ref, g_ref, mod_ref, w_ref, u_ref = refs
    row = pl.program_id(0) if mod_row is None else mod_row
    x = x_ref[0]
    if add_pos:
        x = x + pos_ref[...]
        xo_ref[0] = x
    h = _rms_mod(x, g_ref[...], _mod_row(mod_ref, row, 0), _mod_row(mod_ref, row, 1))
    u_ref[0] = jnp.dot(h.astype(BF16), w_ref[...],
                       preferred_element_type=F32).astype(BF16)


def _in_proj(x, pos, g, mod, w, *, tm, mod_row):
    nb, s, d = x.shape
    n = w.shape[1]
    add_pos = pos is not None
    in_specs = [pl.BlockSpec((1, tm, d), lambda b, t: (b, t, 0))]
    args = [x]
    if add_pos:
        in_specs.append(pl.BlockSpec((tm, d), lambda b, t: (t, 0)))
        args.append(pos)
    in_specs += [
        pl.BlockSpec((1, d), lambda b, t: (0, 0)),
        pl.BlockSpec((SUBLANES, N_MOD * d), lambda b, t: (0, 0)),
        pl.BlockSpec((d, n), lambda b, t: (0, 0)),
    ]
    args += [g.reshape(1, d), mod, w]
    out_specs = [pl.BlockSpec((1, tm, n), lambda b, t: (b, t, 0))]
    out_shape = [jax.ShapeDtypeStruct((nb, s, n), BF16)]
    if add_pos:
        out_specs.append(pl.BlockSpec((1, tm, d), lambda b, t: (b, t, 0)))
        out_shape.append(jax.ShapeDtypeStruct((nb, s, d), F32))
    res = pl.pallas_call(
        functools.partial(_in_kernel, add_pos=add_pos, mod_row=mod_row),
        grid=(nb, s // tm),
        in_specs=in_specs,
        out_specs=out_specs,
        out_shape=out_shape,
        compiler_params=pltpu.CompilerParams(
            dimension_semantics=("arbitrary", "arbitrary"),
            vmem_limit_bytes=VMEM_LIMIT),
        name="in_proj",
    )(*args)
    return (res[0], res[1]) if add_pos else (res[0], x)


def _lru_coeffs(ext_ref, u_ref, prev_ref, next_ref, pvec_ref, wg_ref, has_prev, has_next, ts):
    prev = prev_ref[0].astype(F32)
    nxt = next_ref[0].astype(F32)
    ext_ref[0:SUBLANES, :] = prev[HALO - SUBLANES:HALO, :] * has_prev
    ext_ref[SUBLANES:SUBLANES + ts, :] = u_ref[0].astype(F32)
    ext_ref[SUBLANES + ts:2 * SUBLANES + ts, :] = nxt[0:SUBLANES, :] * has_next
    rc = pvec_ref[4:5, :]
    for k in range(LRU_CONV_W):
        rc = rc + pvec_ref[k:k + 1, :] * ext_ref[pl.ds(SUBLANES - 2 + k, ts), :]
    rcb = rc.astype(BF16)
    ga, gx = [], []
    for j in range(D_LRU // MXU_DIM):
        o = jnp.dot(rcb[:, j * MXU_DIM:(j + 1) * MXU_DIM], wg_ref[j],
                    preferred_element_type=F32)
        ga.append(o[:, :MXU_DIM])
        gx.append(o[:, MXU_DIM:])
    r_gate = _sigmoid(jnp.concatenate(ga, axis=-1) + pvec_ref[5:6, :])
    i_gate = _sigmoid(jnp.concatenate(gx, axis=-1) + pvec_ref[6:7, :])
    neg_lam = -pvec_ref[7:8, :]
    softplus = jnp.maximum(neg_lam, 0.0) + jnp.log1p(jnp.exp(-jnp.abs(neg_lam)))
    log_a = -LRU_C * r_gate * softplus
    a = jnp.exp(log_a)
    one_minus_a2 = -jnp.tanh(log_a) * (a * a + 1.0)
    b = jnp.sqrt(one_minus_a2) * (i_gate * rc)
    return a, b


def _lru_scan(a, b, h_in, hs_ref, ts, reverse):
    groups = ts // SUBLANES
    c = a.shape[-1]
    a3 = a.reshape(groups, SUBLANES, c)
    b3 = b.reshape(groups, SUBLANES, c)
    row = lax.broadcasted_iota(jnp.int32, (groups, SUBLANES, c), 1)
    for s in (1, 2, 4):
        shift = SUBLANES - s if reverse else s
        a_sh = pltpu.roll(a3, shift, axis=1)
        b_sh = pltpu.roll(b3, shift, axis=1)
        m = (row < SUBLANES - s) if reverse else (row >= s)
        b3 = jnp.where(m, a3 * b_sh + b3, b3)
        a3 = jnp.where(m, a3 * a_sh, a3)
    h = h_in
    order = range(groups - 1, -1, -1) if reverse else range(groups)
    edge = 0 if reverse else SUBLANES - 1
    for g in order:
        hg = a3[g] * h + b3[g]
        hs_ref[g * SUBLANES:(g + 1) * SUBLANES, :] = hg
        h = jnp.broadcast_to(hg[edge:edge + 1, :], (SUBLANES, c))
    return h


def _halo_specs(width, col, ts, s, tile_of):
    per = ts // HALO
    last = s // HALO - 1
    main = pl.BlockSpec((1, ts, width), lambda b, t: (b, tile_of(t), col))
    prev = pl.BlockSpec((1, HALO, width),
                        lambda b, t: (b, jnp.maximum(tile_of(t) * per - 1, 0), col))
    nxt = pl.BlockSpec((1, HALO, width),
                       lambda b, t: (b, jnp.minimum((tile_of(t) + 1) * per, last), col))
    return [main, prev, nxt]


def _fwd_kernel(u_ref, prev_ref, next_ref, pvec_ref, wg_ref, seed_ref,
                hf_ref, state_ref, ext_ref, hs_ref, carry_ref, *, ts):
    t = pl.program_id(1)
    nt = pl.num_programs(1)

    @pl.when(t == 0)
    def _():
        carry_ref[...] = seed_ref[0]

    has_prev = (t > 0).astype(F32)
    has_next = (t < nt - 1).astype(F32)
    a, b = _lru_coeffs(ext_ref, u_ref, prev_ref, next_ref, pvec_ref, wg_ref,
                       has_prev, has_next, ts)
    h = _lru_scan(a, b, carry_ref[...], hs_ref, ts, reverse=False)
    carry_ref[...] = h
    state_ref[0] = h
    hf_ref[0] = hs_ref[...].astype(BF16)


def _lru_forward(u, pvec, wg, seed, *, ts):
    nb, s, _ = u.shape
    return pl.pallas_call(
        functools.partial(_fwd_kernel, ts=ts),
        grid=(nb, s // ts),
        in_specs=_halo_specs(D_LRU, 0, ts, s, lambda t: t) + [
            pl.BlockSpec((2 * SUBLANES, D_LRU), lambda b, t: (0, 0)),
            pl.BlockSpec((D_LRU // MXU_DIM, MXU_DIM, 2 * MXU_DIM), lambda b, t: (0, 0, 0)),
            pl.BlockSpec((1, SUBLANES, D_LRU), lambda b, t: (b, 0, 0)),
        ],
        out_specs=[
            pl.BlockSpec((1, ts, D_LRU), lambda b, t: (b, t, 0)),
            pl.BlockSpec((1, SUBLANES, D_LRU), lambda b, t: (b, 0, 0)),
        ],
        out_shape=[
            jax.ShapeDtypeStruct((nb, s, D_LRU), BF16),
            jax.ShapeDtypeStruct((nb, SUBLANES, D_LRU), F32),
        ],
        scratch_shapes=[
            pltpu.VMEM((ts + 2 * SUBLANES, D_LRU), F32),
            pltpu.VMEM((ts, D_LRU), F32),
            pltpu.VMEM((SUBLANES, D_LRU), F32),
        ],
        compiler_params=pltpu.CompilerParams(
            dimension_semantics=("arbitrary", "arbitrary"),
            vmem_limit_bytes=VMEM_LIMIT),
        name="lru_forward",
    )(u, u, u, pvec, wg, seed)


def _bwd_kernel(*refs, ts, merge, mod_row):
    if merge:
        (u_ref, prev_ref, next_ref, pvec_ref, wg_ref, seed_ref,
         gate_ref, v_ref, vprev_ref, vnext_ref, hf_ref, x_ref, mod_ref,
         dww_ref, cvec_ref, wo_ref,
         xo_ref, state_ref,
         ext_ref, hs_ref, carry_ref, g_ref, cs_ref, sh_ref) = refs
    else:
        (u_ref, prev_ref, next_ref, pvec_ref, wg_ref, seed_ref,
         state_ref, ext_ref, hs_ref, carry_ref) = refs
    tg = pl.program_id(1)
    nt = pl.num_programs(1)
    t = nt - 1 - tg

    @pl.when(tg == 0)
    def _():
        carry_ref[...] = seed_ref[0]

    has_prev = (t > 0).astype(F32)
    has_next = (t < nt - 1).astype(F32)
    a, b = _lru_coeffs(ext_ref, u_ref, prev_ref, next_ref, pvec_ref, wg_ref,
                       has_prev, has_next, ts)
    h = _lru_scan(a, b, carry_ref[...], hs_ref, ts, reverse=True)
    carry_ref[...] = h
    state_ref[0] = h
    if not merge:
        return

    y = hf_ref[0].astype(F32) + hs_ref[...]
    gate = gate_ref[0].astype(F32)
    cdf = 0.5 * (1.0 + jnp.tanh(math.sqrt(2.0 / math.pi) * (gate + 0.044715 * (gate * gate * gate))))
    lru = (y * (gate * cdf)).astype(BF16)

    def glu(vref):
        v = vref[0].astype(F32)
        return v[:, :D_CONV] * _sigmoid(v[:, D_CONV:])

    g_ref[0:HALO, :] = glu(vprev_ref) * has_prev
    g_ref[HALO:HALO + ts, :] = glu(v_ref)
    g_ref[HALO + ts:2 * HALO + ts, :] = glu(vnext_ref) * has_next
    pad = (CONF_CONV_W - 1) // 2
    dwb = cvec_ref[0:1, :]
    span = ts + 2 * HALO - SUBLANES
    for r in range(1, SUBLANES):
        sh_ref[r - 1, 0:span, :] = g_ref[pl.ds(r, span), :]

    def conv_chunk(i, carry):
        base = pl.multiple_of(i * CONV_CHUNK, CONV_CHUNK)
        groups = CONV_CHUNK // SUBLANES
        acc = jnp.zeros((groups, SUBLANES, D_CONV), F32) + dwb
        for k in range(CONF_CONV_W):
            q, r = divmod(HALO - pad + k, SUBLANES)
            rows = pl.ds(base + q * SUBLANES, CONV_CHUNK)
            tap = g_ref[rows, :] if r == 0 else sh_ref[r - 1, rows, :]
            wk = dww_ref[k * SUBLANES:(k + 1) * SUBLANES, :]
            acc = acc + wk[None] * tap.reshape(groups, SUBLANES, D_CONV)
        cs_ref[pl.ds(base, CONV_CHUNK), :] = acc.reshape(CONV_CHUNK, D_CONV)
        return carry

    lax.fori_loop(0, ts // CONV_CHUNK, conv_chunk, 0)
    cv = cs_ref[...]
    mu = jnp.mean(cv, axis=-1, keepdims=True)
    var = jnp.mean(jnp.square(cv - mu), axis=-1, keepdims=True)
    ln = (cv - mu) * lax.rsqrt(var + EPS) * cvec_ref[1:2, :] + cvec_ref[2:3, :]
    conf = _silu(ln).astype(BF16)

    o = jnp.dot(lru, wo_ref[0:D_LRU, :], preferred_element_type=F32)
    o = o + jnp.dot(conf, wo_ref[D_LRU:, :], preferred_element_type=F32)
    row = pl.program_id(0) if mod_row is None else mod_row
    xo_ref[0] = x_ref[0] + _mod_row(mod_ref, row, 2) * o


def _lru_backward(u, pvec, wg, seed, *, ts, merge_args=None, mod_row=None):
    nb, s, _ = u.shape
    nt = s // ts
    rev = lambda t: nt - 1 - t
    in_specs = _halo_specs(D_LRU, 0, ts, s, rev) + [
        pl.BlockSpec((2 * SUBLANES, D_LRU), lambda b, t: (0, 0)),
        pl.BlockSpec((D_LRU // MXU_DIM, MXU_DIM, 2 * MXU_DIM), lambda b, t: (0, 0, 0)),
        pl.BlockSpec((1, SUBLANES, D_LRU), lambda b, t: (b, 0, 0)),
    ]
    args = [u, u, u, pvec, wg, seed]
    state_spec = pl.BlockSpec((1, SUBLANES, D_LRU), lambda b, t: (b, 0, 0))
    state_shape = jax.ShapeDtypeStruct((nb, SUBLANES, D_LRU), F32)
    scratch = [
        pltpu.VMEM((ts + 2 * SUBLANES, D_LRU), F32),
        pltpu.VMEM((ts, D_LRU), F32),
        pltpu.VMEM((SUBLANES, D_LRU), F32),
    ]
    merge = merge_args is not None
    if merge:
        hf, x, mod, dww, cvec, wo = merge_args
        d = x.shape[-1]
        in_specs += [pl.BlockSpec((1, ts, D_LRU), lambda b, t: (b, rev(t), 1))]
        in_specs += _halo_specs(2 * D_CONV, 1, ts, s, rev)
        in_specs += [
            pl.BlockSpec((1, ts, D_LRU), lambda b, t: (b, rev(t), 0)),
            pl.BlockSpec((1, ts, d), lambda b, t: (b, rev(t), 0)),
            pl.BlockSpec((SUBLANES, N_MOD * d), lambda b, t: (0, 0)),
            pl.BlockSpec((CONF_CONV_W * SUBLANES, D_CONV), lambda b, t: (0, 0)),
            pl.BlockSpec((SUBLANES, D_CONV), lambda b, t: (0, 0)),
            pl.BlockSpec((d, d), lambda b, t: (0, 0)),
        ]
        args += [u, u, u, u, hf, x, mod, dww, cvec, wo]
        out_specs = [pl.BlockSpec((1, ts, d), lambda b, t: (b, rev(t), 0)), state_spec]
        out_shape = [jax.ShapeDtypeStruct((nb, s, d), F32), state_shape]
        scratch += [
            pltpu.VMEM((ts + 2 * HALO, D_CONV), F32),
            pltpu.VMEM((ts, D_CONV), F32),
            pltpu.VMEM((SUBLANES - 1, ts + 2 * HALO, D_CONV), F32),
        ]
    else:
        out_specs = [state_spec]
        out_shape = [state_shape]
    res = pl.pallas_call(
        functools.partial(_bwd_kernel, ts=ts, merge=merge, mod_row=mod_row),
        grid=(nb, nt),
        in_specs=in_specs,
        out_specs=out_specs,
        out_shape=out_shape,
        scratch_shapes=scratch,
        compiler_params=pltpu.CompilerParams(
            dimension_semantics=("arbitrary", "arbitrary"),
            vmem_limit_bytes=VMEM_LIMIT),
        name="lru_backward_merge" if merge else "lru_backward",
    )(*args)
    return res if merge else (None, res[0])


ROUTE_BLOCK = 512
CHUNK = BF16_ROWS
BLOCK_CAP = 2 * ROUTE_BLOCK + LANES
CHUNKS_PER_BLOCK = BLOCK_CAP // CHUNK
MOE_TILE = 1024
FF_BLOCK = D_FF // 2
FF_SUB = MXU_DIM
CHUNKS_PER_TILE = MOE_TILE // CHUNK


def _route_kernel(x_ref, g_ref, mod_ref, wr_ref, hs_ref, ws_ref, cnt_ref, pos_ref,
                  *, blocks_per_batch, n_lat_blocks):
    i = pl.program_id(0)
    row = jnp.where(i < n_lat_blocks, i // blocks_per_batch, CTX_MOD_ROW)
    h = _rms_mod(x_ref[...], g_ref[...], _mod_row(mod_ref, row, 3), _mod_row(mod_ref, row, 4))
    logits = jnp.dot(h, wr_ref[...], preferred_element_type=F32,
                     precision=lax.Precision.HIGHEST)
    lane = lax.broadcasted_iota(jnp.int32, logits.shape, 1).astype(F32)
    neg = jnp.float32(-jnp.inf)
    l1 = jnp.where(lane < N_EXPERTS, logits, neg)
    m1 = jnp.max(l1, axis=-1, keepdims=True)
    i1 = jnp.min(jnp.where(l1 == m1, lane, float(LANES)), axis=-1, keepdims=True)
    l2 = jnp.where(lane == i1, neg, l1)
    m2 = jnp.max(l2, axis=-1, keepdims=True)
    i2 = jnp.min(jnp.where(l2 == m2, lane, float(LANES)), axis=-1, keepdims=True)
    e = jnp.exp(m2 - m1)
    p1 = 1.0 / (1.0 + e)
    p2 = e / (1.0 + e)

    sel1 = lane == i1
    sel2 = lane == i2
    member = jnp.where(sel1 | sel2, 1.0, 0.0)
    cnt = jnp.sum(member, axis=0, keepdims=True)
    padded = jnp.floor((cnt + (CHUNK - 1.0)) * (1.0 / CHUNK)) * CHUNK
    r128 = lax.broadcasted_iota(jnp.int32, (LANES, LANES), 0)
    c128 = lax.broadcasted_iota(jnp.int32, (LANES, LANES), 1)
    before = jnp.where(r128 < c128, 1.0, 0.0)
    seg_off = jnp.dot(jnp.broadcast_to(padded, (SUBLANES, LANES)), before,
                      preferred_element_type=F32, precision=lax.Precision.HIGHEST)[0:1, :]
    bt = member.shape[0]
    rt = lax.broadcasted_iota(jnp.int32, (bt, bt), 0)
    ct = lax.broadcasted_iota(jnp.int32, (bt, bt), 1)
    earlier = jnp.where(ct < rt, 1.0, 0.0).astype(BF16)
    rank = jnp.dot(earlier, member.astype(BF16), preferred_element_type=F32)
    slot_of = seg_off + rank
    pos1 = jnp.sum(jnp.where(sel1, slot_of, 0.0), axis=-1, keepdims=True)
    pos2 = jnp.sum(jnp.where(sel2, slot_of, 0.0), axis=-1, keepdims=True)
    stats = (jnp.where(lane == 0.0, pos1, 0.0) + jnp.where(lane == 1.0, pos2, 0.0)
             + jnp.where(lane == 2.0, p1, 0.0) + jnp.where(lane == 3.0, p2, 0.0))
    pos_ref[...] = stats
    cnt_ref[0] = jnp.broadcast_to(cnt, (SUBLANES, LANES))

    st = stats.T
    slot = lax.broadcasted_iota(jnp.int32, (BLOCK_CAP, bt), 0).astype(F32)
    hit1 = slot == st[0:1, :]
    hit2 = slot == st[1:2, :]
    perm = jnp.where(hit1 | hit2, 1.0, 0.0).astype(BF16)
    hs_ref[...] = jnp.dot(perm, h.astype(BF16), preferred_element_type=F32).astype(BF16)
    wrow = jnp.where(hit1, st[2:3, :], 0.0) + jnp.where(hit2, st[3:4, :], 0.0)
    ws_ref[...] = jnp.broadcast_to(jnp.sum(wrow, axis=-1, keepdims=True), (BLOCK_CAP, LANES))


def _route(xall, g, mod, wr, *, n_lat_blocks, blocks_per_batch):
    t, d = xall.shape
    nblk = t // ROUTE_BLOCK
    return pl.pallas_call(
        functools.partial(_route_kernel, blocks_per_batch=blocks_per_batch,
                          n_lat_blocks=n_lat_blocks),
        grid=(nblk,),
        in_specs=[
            pl.BlockSpec((ROUTE_BLOCK, d), lambda i: (i, 0)),
            pl.BlockSpec((1, d), lambda i: (0, 0)),
            pl.BlockSpec((SUBLANES, N_MOD * d), lambda i: (0, 0)),
            pl.BlockSpec((d, LANES), lambda i: (0, 0)),
        ],
        out_specs=[
            pl.BlockSpec((BLOCK_CAP, d), lambda i: (i, 0)),
            pl.BlockSpec((BLOCK_CAP, LANES), lambda i: (i, 0)),
            pl.BlockSpec((1, SUBLANES, LANES), lambda i: (i, 0, 0)),
            pl.BlockSpec((ROUTE_BLOCK, LANES), lambda i: (i, 0)),
        ],
        out_shape=[
            jax.ShapeDtypeStruct((nblk * BLOCK_CAP, d), BF16),
            jax.ShapeDtypeStruct((nblk * BLOCK_CAP, LANES), F32),
            jax.ShapeDtypeStruct((nblk, SUBLANES, LANES), F32),
            jax.ShapeDtypeStruct((t, LANES), F32),
        ],
        compiler_params=pltpu.CompilerParams(
            dimension_semantics=("arbitrary",),
            vmem_limit_bytes=VMEM_LIMIT),
        name="route",
    )(xall, g.reshape(1, d), mod, wr)


def _route_tables(cnt, n_tiles_max):
    nblk = cnt.shape[0]
    n_be = (cnt[:, 0, :N_EXPERTS].astype(jnp.int32) + (CHUNK - 1)) // CHUNK
    ends_be = jnp.cumsum(n_be, axis=1)
    off_be = ends_be - n_be
    start_be = jnp.cumsum(n_be, axis=0) - n_be
    g_e = jnp.sum(n_be, axis=0)
    tiles_e = (g_e + (CHUNKS_PER_TILE - 1)) // CHUNKS_PER_TILE
    tend_e = jnp.cumsum(tiles_e)
    tstart_e = tend_e - tiles_e
    n_tiles = tend_e[-1]

    experts = jnp.arange(N_EXPERTS, dtype=jnp.int32)

    def pick(onehot, table):
        return jnp.sum(jnp.where(onehot, table, 0), axis=-1)

    c = jnp.arange(CHUNKS_PER_BLOCK, dtype=jnp.int32)
    e_bc = jnp.sum((c[None, :, None] >= ends_be[:, None, :]).astype(jnp.int32), axis=-1)
    oh_bc = e_bc[:, :, None] == experts
    shift_be = tstart_e[None, :] * CHUNKS_PER_TILE + start_be - off_be
    d_bc = pick(oh_bc, shift_be[:, None, :]) + c[None, :]
    dst = jnp.where(e_bc < N_EXPERTS, d_bc, -1).reshape(-1).astype(jnp.int32)

    ti = jnp.arange(n_tiles_max, dtype=jnp.int32)
    ti_cl = jnp.minimum(ti, n_tiles - 1)
    tile_expert = jnp.minimum(
        jnp.sum((ti_cl[:, None] >= tend_e[None, :]).astype(jnp.int32), axis=-1), N_EXPERTS - 1)
    oh_t = tile_expert[:, None] == experts
    first_q = (ti - pick(oh_t, tstart_e[None, :])) * CHUNKS_PER_TILE
    left = pick(oh_t, g_e[None, :]) - first_q
    tile_nvalid = jnp.where(ti < n_tiles, jnp.clip(left, 0, CHUNKS_PER_TILE), 0).astype(jnp.int32)

    q = first_q[:, None] + jnp.arange(CHUNKS_PER_TILE, dtype=jnp.int32)[None, :]
    cum_tb = pick(oh_t[:, None, :], (start_be + n_be)[None, :, :])
    b_tj = jnp.minimum(jnp.sum((q[:, :, None] >= cum_tb[:, None, :]).astype(jnp.int32), axis=-1),
                       nblk - 1)
    oh_b = b_tj[:, :, None] == jnp.arange(nblk, dtype=jnp.int32)
    local_tb = pick(oh_t[:, None, :], (off_be - start_be)[None, :, :])
    src = b_tj * CHUNKS_PER_BLOCK + q + pick(oh_b, local_tb[:, None, :])
    src = jnp.clip(src, 0, nblk * CHUNKS_PER_BLOCK - 1).reshape(-1).astype(jnp.int32)
    return (tile_expert.astype(jnp.int32), tile_nvalid, src,
            n_tiles.reshape(1).astype(jnp.int32), dst)


def _chunk_dmas(table_ref, base, count, src_hbm, dst_buf, sem, n_slots, *, wait):
    def step(j, carry):
        cid = table_ref[base + j]

        @pl.when((j < count) & (cid >= 0))
        def _():
            copy = pltpu.make_async_copy(
                src_hbm.at[pl.ds(pl.multiple_of(cid * CHUNK, CHUNK), CHUNK), :],
                dst_buf.at[pl.ds(pl.multiple_of(j * CHUNK, CHUNK), CHUNK), :], sem)
            if wait:
                copy.wait()
            else:
                copy.start()
        return carry

    lax.fori_loop(0, n_slots, step, 0)


def _swiglu_accumulate(xb, w1_ref, w3_ref, w2_ref, acc_ref, tf):
    for c0 in range(0, tf, FF_SUB):
        c1 = min(c0 + FF_SUB, tf)
        up = jnp.dot(xb, w1_ref[:, c0:c1].astype(BF16), preferred_element_type=F32)
        lin = jnp.dot(xb, w3_ref[:, c0:c1].astype(BF16), preferred_element_type=F32)
        hid = (_silu(up) * lin).astype(BF16)
        acc_ref[...] += jnp.dot(hid, w2_ref[c0:c1, :].astype(BF16), preferred_element_type=F32)


def _moe_kernel(te_ref, nv_ref, src_ref, nt_ref, hs_hbm, ws_hbm, w1_ref, w3_ref, w2_ref,
                ys_ref, xbuf, wbuf, acc_ref, sems, *, tf):
    i = pl.program_id(0)
    f = pl.program_id(1)
    n_tiles = nt_ref[0]
    valid = i < n_tiles
    slot = lax.rem(i, 2)

    def gather(tile, buf_slot, wait):
        base = tile * CHUNKS_PER_TILE
        _chunk_dmas(src_ref, base, nv_ref[tile], hs_hbm, xbuf.at[buf_slot], sems.at[0, buf_slot],
                    CHUNKS_PER_TILE, wait=wait)
        _chunk_dmas(src_ref, base, nv_ref[tile], ws_hbm, wbuf.at[buf_slot], sems.at[1, buf_slot],
                    CHUNKS_PER_TILE, wait=wait)

    @pl.when((i == 0) & (f == 0))
    def _():
        xbuf[...] = jnp.zeros_like(xbuf)
        wbuf[...] = jnp.zeros_like(wbuf)
        gather(0, 0, wait=False)

    @pl.when(valid & (f == 0))
    def _():
        gather(i, slot, wait=True)

        @pl.when(i + 1 < n_tiles)
        def _():
            gather(i + 1, 1 - slot, wait=False)

        acc_ref[...] = jnp.zeros_like(acc_ref)

    @pl.when(valid)
    def _():
        _swiglu_accumulate(xbuf[slot], w1_ref.at[0, 0], w3_ref.at[0, 0], w2_ref.at[0, 0], acc_ref, tf)

    @pl.when(valid & (f == pl.num_programs(1) - 1))
    def _():
        ys_ref[...] = (acc_ref[...] * wbuf[slot][:, 0:1]).astype(BF16)

    @pl.when(jnp.logical_not(valid) & (f == 0))
    def _():
        ys_ref[...] = jnp.zeros_like(ys_ref)


def _moe_grouped(hs, ws, tables, w1, w3, w2, layer, *, n_tiles_max, tf):
    tile_expert, tile_nvalid, src, n_tiles, _ = tables
    d = hs.shape[1]
    ff = w1.shape[-1]
    nf = ff // tf

    def w_in_map(i, f, te, nv, sr, nt):
        return (layer, te[i], 0, jnp.where(i < nt[0], f, nf - 1))

    def w_out_map(i, f, te, nv, sr, nt):
        return (layer, te[i], jnp.where(i < nt[0], f, nf - 1), 0)

    def y_map(i, f, te, nv, sr, nt):
        return (i, 0)

    grid_spec = pltpu.PrefetchScalarGridSpec(
        num_scalar_prefetch=4,
        grid=(n_tiles_max, nf),
        in_specs=[
            pl.BlockSpec(memory_space=pl.ANY),
            pl.BlockSpec(memory_space=pl.ANY),
            pl.BlockSpec((1, 1, d, tf), w_in_map),
            pl.BlockSpec((1, 1, d, tf), w_in_map),
            pl.BlockSpec((1, 1, tf, d), w_out_map),
        ],
        out_specs=pl.BlockSpec((MOE_TILE, d), y_map),
        scratch_shapes=[
            pltpu.VMEM((2, MOE_TILE, d), BF16),
            pltpu.VMEM((2, MOE_TILE, LANES), F32),
            pltpu.VMEM((MOE_TILE, d), F32),
            pltpu.SemaphoreType.DMA((2, 2)),
        ],
    )
    return pl.pallas_call(
        functools.partial(_moe_kernel, tf=tf),
        grid_spec=grid_spec,
        out_shape=jax.ShapeDtypeStruct((n_tiles_max * MOE_TILE, d), BF16),
        compiler_params=pltpu.CompilerParams(
            dimension_semantics=("arbitrary", "arbitrary"),
            vmem_limit_bytes=VMEM_LIMIT),
        name="moe_grouped",
    )(tile_expert, tile_nvalid, src, n_tiles, hs, ws, w1, w3, w2)


def _combine_kernel(dst_ref, ys_hbm, pos_ref, x_ref, fg_ref, mod_ref, o_ref, ybuf, sem,
                    *, blocks_per_batch, n_lat_blocks, final):
    i = pl.program_id(0)
    row = jnp.where(i < n_lat_blocks, i // blocks_per_batch, CTX_MOD_ROW)

    @pl.when(i == 0)
    def _():
        ybuf[...] = jnp.zeros_like(ybuf)

    gather = functools.partial(_chunk_dmas, dst_ref, i * CHUNKS_PER_BLOCK, CHUNKS_PER_BLOCK,
                               ys_hbm, ybuf, sem.at[0], CHUNKS_PER_BLOCK)
    gather(wait=False)
    stats = pos_ref[...]
    slot = lax.broadcasted_iota(jnp.int32, (stats.shape[0], BLOCK_CAP), 1).astype(F32)
    pick = jnp.where((slot == stats[:, 0:1]) | (slot == stats[:, 1:2]), 1.0, 0.0).astype(BF16)
    gather(wait=True)
    moe = jnp.dot(pick, ybuf[...], preferred_element_type=F32)
    y = x_ref[...] + _mod_row(mod_ref, row, 5) * moe
    if final:
        ms = jnp.mean(y * y, axis=-1, keepdims=True)
        y = y * lax.rsqrt(ms + EPS) * fg_ref[...]
    o_ref[...] = y


def _combine(ys, dst, pos, xall, fg, mod, *, n_lat_blocks, blocks_per_batch, final):
    t, d = xall.shape
    nblk = t // ROUTE_BLOCK
    grid_spec = pltpu.PrefetchScalarGridSpec(
        num_scalar_prefetch=1,
        grid=(nblk,),
        in_specs=[
            pl.BlockSpec(memory_space=pl.ANY),
            pl.BlockSpec((ROUTE_BLOCK, LANES), lambda i, ds: (i, 0)),
            pl.BlockSpec((ROUTE_BLOCK, d), lambda i, ds: (i, 0)),
            pl.BlockSpec((1, d), lambda i, ds: (0, 0)),
            pl.BlockSpec((SUBLANES, N_MOD * d), lambda i, ds: (0, 0)),
        ],
        out_specs=pl.BlockSpec((ROUTE_BLOCK, d), lambda i, ds: (i, 0)),
        scratch_shapes=[
            pltpu.VMEM((BLOCK_CAP, d), BF16),
            pltpu.SemaphoreType.DMA((1,)),
        ],
    )
    return pl.pallas_call(
        functools.partial(_combine_kernel, blocks_per_batch=blocks_per_batch,
                          n_lat_blocks=n_lat_blocks, final=final),
        grid_spec=grid_spec,
        out_shape=jax.ShapeDtypeStruct((t, d), F32),
        compiler_params=pltpu.CompilerParams(
            dimension_semantics=("arbitrary",),
            vmem_limit_bytes=VMEM_LIMIT),
        name="moe_combine",
    )(dst, ys, pos, xall, fg.reshape(1, d), mod)


def _ffn_kernel(x_ref, g_ref, mod_ref, w1_ref, w3_ref, w2_ref, o_ref, h_ref, acc_ref,
                *, tiles_per_batch, mod_row, tf):
    i = pl.program_id(0)
    f = pl.program_id(1)
    row = i // tiles_per_batch if mod_row is None else mod_row

    @pl.when(f == 0)
    def _():
        h = _rms_mod(x_ref[...], g_ref[...], _mod_row(mod_ref, row, 3), _mod_row(mod_ref, row, 4))
        h_ref[...] = h.astype(BF16)
        acc_ref[...] = jnp.zeros_like(acc_ref)

    _swiglu_accumulate(h_ref[...], w1_ref.at[0], w3_ref.at[0], w2_ref.at[0], acc_ref, tf)

    @pl.when(f == pl.num_programs(1) - 1)
    def _():
        o_ref[...] = x_ref[...] + _mod_row(mod_ref, row, 5) * acc_ref[...]


def _ffn(x2, g, mod, w1, w3, w2, layer, *, tm, tf, tiles_per_batch, mod_row):
    t, d = x2.shape
    ff = w1.shape[-1]
    return pl.pallas_call(
        functools.partial(_ffn_kernel, tiles_per_batch=tiles_per_batch, mod_row=mod_row, tf=tf),
        grid=(t // tm, ff // tf),
        in_specs=[
            pl.BlockSpec((tm, d), lambda i, f: (i, 0)),
            pl.BlockSpec((1, d), lambda i, f: (0, 0)),
            pl.BlockSpec((SUBLANES, N_MOD * d), lambda i, f: (0, 0)),
            pl.BlockSpec((1, d, tf), lambda i, f: (layer, 0, f)),
            pl.BlockSpec((1, d, tf), lambda i, f: (layer, 0, f)),
            pl.BlockSpec((1, tf, d), lambda i, f: (layer, f, 0)),
        ],
        out_specs=pl.BlockSpec((tm, d), lambda i, f: (i, 0)),
        out_shape=jax.ShapeDtypeStruct((t, d), F32),
        scratch_shapes=[pltpu.VMEM((tm, d), BF16), pltpu.VMEM((tm, d), F32)],
        compiler_params=pltpu.CompilerParams(
            dimension_semantics=("arbitrary", "arbitrary"),
            vmem_limit_bytes=VMEM_LIMIT),
        name="dense_ffn",
    )(x2, g.reshape(1, d), mod, w1, w3, w2)


def _sincos_2d(rows, cols, dim):
    quarter = dim // 4
    omega = 1.0 / (10000.0 ** (jnp.arange(quarter, dtype=F32) / quarter))

    def emb1d(n):
        ang = jnp.arange(n, dtype=F32)[:, None] * omega[None, :]
        return jnp.concatenate([jnp.sin(ang), jnp.cos(ang)], axis=-1)

    er = jnp.broadcast_to(emb1d(rows)[:, None, :], (rows, cols, dim // 2))
    ec = jnp.broadcast_to(emb1d(cols)[None, :, :], (rows, cols, dim // 2))
    return jnp.concatenate([er, ec], axis=-1).reshape(rows * cols, dim)


def _block_diag_tiles(w):
    per = MXU_DIM // LRU_HEAD_DIM
    w4 = w.reshape(D_LRU // MXU_DIM, per, LRU_HEAD_DIM, LRU_HEAD_DIM)
    eye = jnp.eye(per, dtype=w.dtype)
    return jnp.einsum("ghij,hk->ghikj", w4, eye).reshape(D_LRU // MXU_DIM, MXU_DIM, MXU_DIM)


def _gate_weights(wa, wx):
    return jnp.concatenate([_block_diag_tiles(wa), _block_diag_tiles(wx)], axis=-1).astype(BF16)


def _lru_vectors(conv_w, conv_b, ba, bx, lam):
    rows = jnp.concatenate([conv_w, conv_b[None], ba[None], bx[None], lam[None]], axis=0)
    return jnp.pad(rows, ((0, 2 * SUBLANES - rows.shape[0]), (0, 0)))


def kernel(x, c, ctx, c_ctx, w_mod, b_mod, norm1_g, norm2_g, w_in, w_out, lru_conv_w, lru_conv_b, lru_wa, lru_ba, lru_wx, lru_bx, lru_lambda, conf_dw_w, conf_dw_b, conf_ln_g, conf_ln_b, ffn_w1, ffn_w3, ffn_w2, moe_router, moe_w1, moe_w3, moe_w2, final_g):
    nb, s, d = x.shape
    n_ctx = ctx.shape[1]
    pos = _sincos_2d(s // GRID_W, GRID_W, d)
    cc = jnp.concatenate([c, c_ctx[None], jnp.zeros((SUBLANES - nb - 1, d), F32)], axis=0)
    mod_all = _modulation(cc, w_mod, b_mod)

    ts_x, ts_c = 512, n_ctx
    tm_x = 1024
    tiles_per_batch = s // tm_x
    zero_state = jnp.zeros((nb, SUBLANES, D_LRU), F32)
    xc = ctx.reshape(1, nb * n_ctx, d)

    ffn_w1_b, ffn_w3_b, ffn_w2_b = (w.astype(BF16) for w in (ffn_w1, ffn_w3, ffn_w2))

    for l in range(DEPTH):
        last = l == DEPTH - 1
        mod = mod_all[l]
        w_in_b = w_in[l].astype(BF16)
        w_out_b = w_out[l].astype(BF16)
        pvec = [_lru_vectors(lru_conv_w[l], lru_conv_b[l], lru_ba[l, dr], lru_bx[l, dr],
                             lru_lambda[l, dr]) for dr in range(2)]
        wg = [_gate_weights(lru_wa[l, dr], lru_wx[l, dr]) for dr in range(2)]
        dww = jnp.repeat(conf_dw_w[l], SUBLANES, axis=0)
        cvec = jnp.pad(jnp.stack([conf_dw_b[l], conf_ln_g[l], conf_ln_b[l]]),
                       ((0, SUBLANES - 3), (0, 0)))

        u_x, x = _in_proj(x, pos if l == 0 else None, norm1_g[l], mod, w_in_b,
                          tm=tm_x, mod_row=None)
        u_c, _ = _in_proj(xc, None, norm1_g[l], mod, w_in_b[:, :D_LRU] if last else w_in_b,
                          tm=512, mod_row=CTX_MOD_ROW)
        u_c = u_c.reshape(nb, n_ctx, u_c.shape[-1])
        hf_c, st_f = _lru_forward(u_c, pvec[0], wg[0], zero_state, ts=ts_c)
        hf_x, _ = _lru_forward(u_x, pvec[0], wg[0], st_f, ts=ts_x)
        if last:
            _, st_b = _lru_backward(u_c, pvec[1], wg[1], zero_state, ts=ts_c)
        else:
            xc_new, st_b = _lru_backward(
                u_c, pvec[1], wg[1], zero_state, ts=ts_c,
                merge_args=(hf_c, xc.reshape(nb, n_ctx, d), mod, dww, cvec, w_out_b),
                mod_row=CTX_MOD_ROW)
            xc = xc_new.reshape(1, nb * n_ctx, d)
        x, _ = _lru_backward(u_x, pvec[1], wg[1], st_b, ts=ts_x,
                             merge_args=(hf_x, x, mod, dww, cvec, w_out_b), mod_row=None)

        j = l // 2
        x2 = x.reshape(nb * s, d)
        if l % 2 == 0:
            x2 = _ffn(x2, norm2_g[l], mod, ffn_w1_b, ffn_w3_b, ffn_w2_b, j, tm=tm_x, tf=FF_BLOCK,
                      tiles_per_batch=tiles_per_batch, mod_row=None)
            xc2 = _ffn(xc[0], norm2_g[l], mod, ffn_w1_b, ffn_w3_b, ffn_w2_b, j, tm=512, tf=FF_BLOCK,
                       tiles_per_batch=1, mod_row=CTX_MOD_ROW)
            xc = xc2.reshape(1, nb * n_ctx, d)
        else:
            xall = x2 if last else jnp.concatenate([x2, xc[0]], axis=0)
            nblk = xall.shape[0] // ROUTE_BLOCK
            n_lat_blocks = x2.shape[0] // ROUTE_BLOCK
            blocks_per_batch = s // ROUTE_BLOCK
            n_tiles_max = (nblk * CHUNKS_PER_BLOCK + N_EXPERTS * (CHUNKS_PER_TILE - 1)) // CHUNKS_PER_TILE
            wr = jnp.pad(moe_router[j], ((0, 0), (0, LANES - N_EXPERTS)))
            hs, ws, cnt, pos_tok = _route(xall, norm2_g[l], mod, wr, n_lat_blocks=n_lat_blocks,
                                          blocks_per_batch=blocks_per_batch)
            tables = _route_tables(cnt, n_tiles_max)
            ys = _moe_grouped(hs, ws, tables, moe_w1, moe_w3, moe_w2, j,
                              n_tiles_max=n_tiles_max, tf=FF_BLOCK)
            xall = _combine(ys, tables[4], pos_tok, xall, final_g, mod, n_lat_blocks=n_lat_blocks,
                            blocks_per_batch=blocks_per_batch, final=last)
            x2 = xall[:x2.shape[0]]
            if not last:
                xc = xall[x2.shape[0]:].reshape(1, nb * n_ctx, d)
        x = x2.reshape(nb, s, d)
    return x
```

```python
import functools
import math

import jax
import jax.numpy as jnp
from jax import lax
from jax.experimental import pallas as pl
from jax.experimental.pallas import tpu as pltpu

F32 = jnp.float32
BF16 = jnp.bfloat16

D_MODEL = 1024
DEPTH = 4
GRID_W = 64
D_LRU = 512
D_CONV = 512
LRU_HEADS = 8
LRU_HEAD_DIM = 64
LRU_CONV_W = 4
LRU_C = 8.0
CONF_CONV_W = 31
D_FF = 2816
N_EXPERTS = 8
N_MOD = 6
EPS = 1e-6

LANES = 128
SUBLANES = 8
BF16_ROWS = 16
MXU_DIM = 256
VMEM_LIMIT = 60 * 1024 * 1024

HALO = BF16_ROWS
CONV_CHUNK = 32
CTX_MOD_ROW = 4


def _sigmoid(v):
    return 0.5 * jnp.tanh(0.5 * v) + 0.5


def _silu(v):
    h = 0.5 * v
    return h + h * jnp.tanh(h)


def _rms_mod(x, g, shift, scale):
    ms = jnp.mean(x * x, axis=-1, keepdims=True)
    y = x * lax.rsqrt(ms + EPS) * g
    return y * (1.0 + scale) + shift


def _mod_row(mod_ref, row, k):
    r = mod_ref[pl.ds(row, 1), :]
    return r[:, k * D_MODEL:(k + 1) * D_MODEL]


def _mod_kernel(cc_ref, w_ref, b_ref, o_ref):
    s = _silu(cc_ref[...])
    o_ref[0] = jnp.dot(s.astype(BF16), w_ref[0].astype(BF16),
                       preferred_element_type=F32) + b_ref[0]


def _modulation(cc, w_mod, b_mod):
    n_out = N_MOD * D_MODEL
    tn = 1536
    return pl.pallas_call(
        _mod_kernel,
        grid=(DEPTH, n_out // tn),
        in_specs=[
            pl.BlockSpec((SUBLANES, D_MODEL), lambda l, j: (0, 0)),
            pl.BlockSpec((1, D_MODEL, tn), lambda l, j: (l, 0, j)),
            pl.BlockSpec((1, 1, tn), lambda l, j: (l, 0, j)),
        ],
        out_specs=pl.BlockSpec((1, SUBLANES, tn), lambda l, j: (l, 0, j)),
        out_shape=jax.ShapeDtypeStruct((DEPTH, SUBLANES, n_out), F32),
        compiler_params=pltpu.CompilerParams(
            dimension_semantics=("arbitrary", "arbitrary"),
            vmem_limit_bytes=VMEM_LIMIT),
        name="modulation",
    )(cc, w_mod, b_mod.reshape(DEPTH, 1, n_out))


def _in_kernel(*refs, add_pos, mod_row):
    if add_pos:
        x_ref, pos_ref, g_ref, mod_ref, w_ref, u_ref, xo_ref = refs
    else:
        x_ref, g_ref, mod_ref, w_ref, u_ref = refs
    row = pl.program_id(0) if mod_row is None else mod_row
    x = x_ref[0]
    if add_pos:
        x = x + pos_ref[...]
        xo_ref[0] = x
    h = _rms_mod(x, g_ref[...], _mod_row(mod_ref, row, 0), _mod_row(mod_ref, row, 1))
    u_ref[0] = jnp.dot(h.astype(BF16), w_ref[...],
                       preferred_element_type=F32).astype(BF16)


def _in_proj(x, pos, g, mod, w, *, tm, mod_row):
    nb, s, d = x.shape
    n = w.shape[1]
    add_pos = pos is not None
    in_specs = [pl.BlockSpec((1, tm, d), lambda b, t: (b, t, 0))]
    args = [x]
    if add_pos:
        in_specs.append(pl.BlockSpec((tm, d), lambda b, t: (t, 0)))
        args.append(pos)
    in_specs += [
        pl.BlockSpec((1, d), lambda b, t: (0, 0)),
        pl.BlockSpec((SUBLANES, N_MOD * d), lambda b, t: (0, 0)),
        pl.BlockSpec((d, n), lambda b, t: (0, 0)),
    ]
    args += [g.reshape(1, d), mod, w]
    out_specs = [pl.BlockSpec((1, tm, n), lambda b, t: (b, t, 0))]
    out_shape = [jax.ShapeDtypeStruct((nb, s, n), BF16)]
    if add_pos:
        out_specs.append(pl.BlockSpec((1, tm, d), lambda b, t: (b, t, 0)))
        out_shape.append(jax.ShapeDtypeStruct((nb, s, d), F32))
    res = pl.pallas_call(
        functools.partial(_in_kernel, add_pos=add_pos, mod_row=mod_row),
        grid=(nb, s // tm),
        in_specs=in_specs,
        out_specs=out_specs,
        out_shape=out_shape,
        compiler_params=pltpu.CompilerParams(
            dimension_semantics=("arbitrary", "arbitrary"),
            vmem_limit_bytes=VMEM_LIMIT),
        name="in_proj",
    )(*args)
    return (res[0], res[1]) if add_pos else (res[0], x)


def _short_conv(u_ref, prev_ref, next_ref, pvec_ref, has_prev, has_next, ts):
    prev = prev_ref[0].astype(F32)[HALO - SUBLANES:HALO, :] * has_prev
    nxt = next_ref[0].astype(F32)[0:SUBLANES, :] * has_next
    groups = ts // SUBLANES
    ext = jnp.concatenate([prev, u_ref[0].astype(F32), nxt], axis=0)
    x3 = ext.reshape(groups + 2, SUBLANES, D_LRU)
    row = lax.broadcasted_iota(jnp.int32, (groups, SUBLANES, D_LRU), 1)

    def window(off):
        if off == 0:
            return x3[1:groups + 1]
        r = pltpu.roll(x3, (-off) % SUBLANES, axis=1)
        if off < 0:
            return jnp.where(row >= -off, r[1:groups + 1], r[0:groups])
        return jnp.where(row < SUBLANES - off, r[1:groups + 1], r[2:groups + 2])

    rc = jnp.zeros((groups, SUBLANES, D_LRU), F32) + pvec_ref[4:5, :]
    for k in range(LRU_CONV_W):
        rc = rc + pvec_ref[k:k + 1, :] * window(k - 2)
    return rc.reshape(ts, D_LRU)


def _lru_gates(rc, pvec_ref, wg_ref):
    rcb = rc.astype(BF16)
    ga, gx = [], []
    for j in range(D_LRU // MXU_DIM):
        o = jnp.dot(rcb[:, j * MXU_DIM:(j + 1) * MXU_DIM], wg_ref[j],
                    preferred_element_type=F32)
        ga.append(o[:, :MXU_DIM])
        gx.append(o[:, MXU_DIM:])
    r_gate = _sigmoid(jnp.concatenate(ga, axis=-1) + pvec_ref[5:6, :])
    i_gate = _sigmoid(jnp.concatenate(gx, axis=-1) + pvec_ref[6:7, :])
    neg_lam = -pvec_ref[7:8, :]
    softplus = jnp.maximum(neg_lam, 0.0) + jnp.log1p(jnp.exp(-jnp.abs(neg_lam)))
    log_a = -LRU_C * r_gate * softplus
    a = jnp.exp(log_a)
    one_minus_a2 = -jnp.tanh(log_a) * (a * a + 1.0)
    b = jnp.sqrt(one_minus_a2) * (i_gate * rc)
    return a, b


def _lru_scan(a, b, h_in, hs_ref, ts, reverse):
    groups = ts // SUBLANES
    c = a.shape[-1]
    a3 = a.reshape(groups, SUBLANES, c)
    b3 = b.reshape(groups, SUBLANES, c)
    row = lax.broadcasted_iota(jnp.int32, (groups, SUBLANES, c), 1)
    for s in (1, 2, 4):
        shift = SUBLANES - s if reverse else s
        a_sh = pltpu.roll(a3, shift, axis=1)
        b_sh = pltpu.roll(b3, shift, axis=1)
        m = (row < SUBLANES - s) if reverse else (row >= s)
        b3 = jnp.where(m, a3 * b_sh + b3, b3)
        a3 = jnp.where(m, a3 * a_sh, a3)
    h = h_in
    order = range(groups - 1, -1, -1) if reverse else range(groups)
    edge = 0 if reverse else SUBLANES - 1
    for g in order:
        hg = a3[g] * h + b3[g]
        hs_ref[g * SUBLANES:(g + 1) * SUBLANES, :] = hg
        h = jnp.broadcast_to(hg[edge:edge + 1, :], (SUBLANES, c))
    return h


def _halo_specs(width, col, ts, s, tile_of):
    per = ts // HALO
    last = s // HALO - 1
    main = pl.BlockSpec((1, ts, width), lambda b, t: (b, tile_of(t), col))
    prev = pl.BlockSpec((1, HALO, width),
                        lambda b, t: (b, jnp.maximum(tile_of(t) * per - 1, 0), col))
    nxt = pl.BlockSpec((1, HALO, width),
                       lambda b, t: (b, jnp.minimum((tile_of(t) + 1) * per, last), col))
    return [main, prev, nxt]


def _fwd_kernel(u_ref, prev_ref, next_ref, pvec_ref, wg_ref, seed_ref,
                hf_ref, rc_ref, state_ref, hs_ref, carry_ref, *, ts):
    t = pl.program_id(1)
    nt = pl.num_programs(1)

    @pl.when(t == 0)
    def _():
        carry_ref[...] = seed_ref[0]

    has_prev = (t > 0).astype(F32)
    has_next = (t < nt - 1).astype(F32)
    rc = _short_conv(u_ref, prev_ref, next_ref, pvec_ref, has_prev, has_next, ts)
    rc_ref[0] = rc
    a, b = _lru_gates(rc, pvec_ref, wg_ref)
    h = _lru_scan(a, b, carry_ref[...], hs_ref, ts, reverse=False)
    carry_ref[...] = h
    state_ref[0] = h
    hf_ref[0] = hs_ref[...].astype(BF16)


def _lru_forward(u, pvec, wg, seed, *, ts):
    nb, s, _ = u.shape
    return pl.pallas_call(
        functools.partial(_fwd_kernel, ts=ts),
        grid=(nb, s // ts),
        in_specs=_halo_specs(D_LRU, 0, ts, s, lambda t: t) + [
            pl.BlockSpec((2 * SUBLANES, D_LRU), lambda b, t: (0, 0)),
            pl.BlockSpec((D_LRU // MXU_DIM, MXU_DIM, 2 * MXU_DIM), lambda b, t: (0, 0, 0)),
            pl.BlockSpec((1, SUBLANES, D_LRU), lambda b, t: (b, 0, 0)),
        ],
        out_specs=[
            pl.BlockSpec((1, ts, D_LRU), lambda b, t: (b, t, 0)),
            pl.BlockSpec((1, ts, D_LRU), lambda b, t: (b, t, 0)),
            pl.BlockSpec((1, SUBLANES, D_LRU), lambda b, t: (b, 0, 0)),
        ],
        out_shape=[
            jax.ShapeDtypeStruct((nb, s, D_LRU), BF16),
            jax.ShapeDtypeStruct((nb, s, D_LRU), F32),
            jax.ShapeDtypeStruct((nb, SUBLANES, D_LRU), F32),
        ],
        scratch_shapes=[
            pltpu.VMEM((ts, D_LRU), F32),
            pltpu.VMEM((SUBLANES, D_LRU), F32),
        ],
        compiler_params=pltpu.CompilerParams(
            dimension_semantics=("arbitrary", "arbitrary"),
            vmem_limit_bytes=VMEM_LIMIT),
        name="lru_forward",
    )(u, u, u, pvec, wg, seed)


def _bwd_kernel(*refs, ts, merge, mod_row):
    if merge:
        (rc_ref, pvec_ref, wg_ref, seed_ref,
         gate_ref, v_ref, vprev_ref, vnext_ref, hf_ref, x_ref, mod_ref,
         dww_ref, cvec_ref, wo_ref,
         xo_ref, state_ref,
         hs_ref, carry_ref, g_ref, cs_ref, sh_ref) = refs
    else:
        (rc_ref, pvec_ref, wg_ref, seed_ref,
         state_ref, hs_ref, carry_ref) = refs
    tg = pl.program_id(1)
    nt = pl.num_programs(1)
    t = nt - 1 - tg

    @pl.when(tg == 0)
    def _():
        carry_ref[...] = seed_ref[0]

    has_prev = (t > 0).astype(F32)
    has_next = (t < nt - 1).astype(F32)
    a, b = _lru_gates(rc_ref[0], pvec_ref, wg_ref)
    h = _lru_scan(a, b, carry_ref[...], hs_ref, ts, reverse=True)
    carry_ref[...] = h
    state_ref[0] = h
    if not merge:
        return

    y = hf_ref[0].astype(F32) + hs_ref[...]
    gate = gate_ref[0].astype(F32)
    cdf = 0.5 * (1.0 + jnp.tanh(math.sqrt(2.0 / math.pi) * (gate + 0.044715 * (gate * gate * gate))))
    lru = (y * (gate * cdf)).astype(BF16)

    def glu(vref):
        v = vref[0].astype(F32)
        return v[:, :D_CONV] * _sigmoid(v[:, D_CONV:])

    g_ref[0:HALO, :] = glu(vprev_ref) * has_prev
    g_ref[HALO:HALO + ts, :] = glu(v_ref)
    g_ref[HALO + ts:2 * HALO + ts, :] = glu(vnext_ref) * has_next
    pad = (CONF_CONV_W - 1) // 2
    dwb = cvec_ref[0:1, :]
    span = ts + 2 * HALO - SUBLANES
    for r in range(1, SUBLANES):
        sh_ref[r - 1, 0:span, :] = g_ref[pl.ds(r, span), :]

    def conv_chunk(i, carry):
        base = pl.multiple_of(i * CONV_CHUNK, CONV_CHUNK)
        groups = CONV_CHUNK // SUBLANES
        acc = jnp.zeros((groups, SUBLANES, D_CONV), F32) + dwb
        for k in range(CONF_CONV_W):
            q, r = divmod(HALO - pad + k, SUBLANES)
            rows = pl.ds(base + q * SUBLANES, CONV_CHUNK)
            tap = g_ref[rows, :] if r == 0 else sh_ref[r - 1, rows, :]
            wk = dww_ref[k * SUBLANES:(k + 1) * SUBLANES, :]
            acc = acc + wk[None] * tap.reshape(groups, SUBLANES, D_CONV)
        cs_ref[pl.ds(base, CONV_CHUNK), :] = acc.reshape(CONV_CHUNK, D_CONV)
        return carry

    lax.fori_loop(0, ts // CONV_CHUNK, conv_chunk, 0)
    cv = cs_ref[...]
    mu = jnp.mean(cv, axis=-1, keepdims=True)
    var = jnp.mean(jnp.square(cv - mu), axis=-1, keepdims=True)
    ln = (cv - mu) * lax.rsqrt(var + EPS) * cvec_ref[1:2, :] + cvec_ref[2:3, :]
    conf = _silu(ln).astype(BF16)

    o = jnp.dot(lru, wo_ref[0:D_LRU, :], preferred_element_type=F32)
    o = o + jnp.dot(conf, wo_ref[D_LRU:, :], preferred_element_type=F32)
    row = pl.program_id(0) if mod_row is None else mod_row
    xo_ref[0] = x_ref[0] + _mod_row(mod_ref, row, 2) * o


def _lru_backward(rc, pvec, wg, seed, *, ts, merge_args=None, mod_row=None):
    nb, s, _ = rc.shape
    nt = s // ts
    rev = lambda t: nt - 1 - t
    in_specs = [
        pl.BlockSpec((1, ts, D_LRU), lambda b, t: (b, rev(t), 0)),
        pl.BlockSpec((2 * SUBLANES, D_LRU), lambda b, t: (0, 0)),
        pl.BlockSpec((D_LRU // MXU_DIM, MXU_DIM, 2 * MXU_DIM), lambda b, t: (0, 0, 0)),
        pl.BlockSpec((1, SUBLANES, D_LRU), lambda b, t: (b, 0, 0)),
    ]
    args = [rc, pvec, wg, seed]
    state_spec = pl.BlockSpec((1, SUBLANES, D_LRU), lambda b, t: (b, 0, 0))
    state_shape = jax.ShapeDtypeStruct((nb, SUBLANES, D_LRU), F32)
    scratch = [
        pltpu.VMEM((ts, D_LRU), F32),
        pltpu.VMEM((SUBLANES, D_LRU), F32),
    ]
    merge = merge_args is not None
    if merge:
        u, hf, x, mod, dww, cvec, wo = merge_args
        d = x.shape[-1]
        in_specs += [pl.BlockSpec((1, ts, D_LRU), lambda b, t: (b, rev(t), 1))]
        in_specs += _halo_specs(2 * D_CONV, 1, ts, s, rev)
        in_specs += [
            pl.BlockSpec((1, ts, D_LRU), lambda b, t: (b, rev(t), 0)),
            pl.BlockSpec((1, ts, d), lambda b, t: (b, rev(t), 0)),
            pl.BlockSpec((SUBLANES, N_MOD * d), lambda b, t: (0, 0)),
            pl.BlockSpec((CONF_CONV_W * SUBLANES, D_CONV), lambda b, t: (0, 0)),
            pl.BlockSpec((SUBLANES, D_CONV), lambda b, t: (0, 0)),
            pl.BlockSpec((d, d), lambda b, t: (0, 0)),
        ]
        args += [u, u, u, u, hf, x, mod, dww, cvec, wo]
        out_specs = [pl.BlockSpec((1, ts, d), lambda b, t: (b, rev(t), 0)), state_spec]
        out_shape = [jax.ShapeDtypeStruct((nb, s, d), F32), state_shape]
        scratch += [
            pltpu.VMEM((ts + 2 * HALO, D_CONV), F32),
            pltpu.VMEM((ts, D_CONV), F32),
            pltpu.VMEM((SUBLANES - 1, ts + 2 * HALO, D_CONV), F32),
        ]
    else:
        out_specs = [state_spec]
        out_shape = [state_shape]
    res = pl.pallas_call(
        functools.partial(_bwd_kernel, ts=ts, merge=merge, mod_row=mod_row),
        grid=(nb, nt),
        in_specs=in_specs,
        out_specs=out_specs,
        out_shape=out_shape,
        scratch_shapes=scratch,
        compiler_params=pltpu.CompilerParams(
            dimension_semantics=("arbitrary", "arbitrary"),
            vmem_limit_bytes=VMEM_LIMIT),
        name="lru_backward_merge" if merge else "lru_backward",
    )(*args)
    return res if merge else (None, res[0])


ROUTE_BLOCK = 512
CHUNK = BF16_ROWS
BLOCK_CAP = 2 * ROUTE_BLOCK + LANES
CHUNKS_PER_BLOCK = BLOCK_CAP // CHUNK
MOE_TILE = 1024
MOE_PART = MOE_TILE // 2
FF_BLOCK = D_FF // 2
FF_SUB = MXU_DIM
CHUNKS_PER_TILE = MOE_TILE // CHUNK


def _route_kernel(x_ref, g_ref, mod_ref, wr_ref, hs_ref, ws_ref, cnt_ref, pos_ref,
                  *, blocks_per_batch, mod_row):
    i = pl.program_id(0)
    row = i // blocks_per_batch if mod_row is None else mod_row
    h = _rms_mod(x_ref[...], g_ref[...], _mod_row(mod_ref, row, 3), _mod_row(mod_ref, row, 4))
    wr = wr_ref[...]
    h_hi = h.astype(BF16)
    h_lo = (h - h_hi.astype(F32)).astype(BF16)
    w_hi = wr.astype(BF16)
    w_lo = (wr - w_hi.astype(F32)).astype(BF16)
    logits = (jnp.dot(h_hi, w_hi, preferred_element_type=F32)
              + jnp.dot(h_hi, w_lo, preferred_element_type=F32)
              + jnp.dot(h_lo, w_hi, preferred_element_type=F32))
    lane = lax.broadcasted_iota(jnp.int32, logits.shape, 1).astype(F32)
    neg = jnp.float32(-jnp.inf)
    l1 = jnp.where(lane < N_EXPERTS, logits, neg)
    m1 = jnp.max(l1, axis=-1, keepdims=True)
    i1 = jnp.min(jnp.where(l1 == m1, lane, float(LANES)), axis=-1, keepdims=True)
    l2 = jnp.where(lane == i1, neg, l1)
    m2 = jnp.max(l2, axis=-1, keepdims=True)
    i2 = jnp.min(jnp.where(l2 == m2, lane, float(LANES)), axis=-1, keepdims=True)
    e = jnp.exp(m2 - m1)
    p1 = 1.0 / (1.0 + e)
    p2 = e / (1.0 + e)

    sel1 = lane == i1
    sel2 = lane == i2
    member = jnp.where(sel1 | sel2, 1.0, 0.0)
    cnt = jnp.sum(member, axis=0, keepdims=True)
    padded = jnp.floor((cnt + (CHUNK - 1.0)) * (1.0 / CHUNK)) * CHUNK
    r128 = lax.broadcasted_iota(jnp.int32, (LANES, LANES), 0)
    c128 = lax.broadcasted_iota(jnp.int32, (LANES, LANES), 1)
    before = jnp.where(r128 < c128, 1.0, 0.0)
    seg_off = jnp.dot(jnp.broadcast_to(padded, (SUBLANES, LANES)), before,
                      preferred_element_type=F32, precision=lax.Precision.HIGHEST)[0:1, :]
    bt = member.shape[0]
    rt = lax.broadcasted_iota(jnp.int32, (bt, bt), 0)
    ct = lax.broadcasted_iota(jnp.int32, (bt, bt), 1)
    earlier = jnp.where(ct < rt, 1.0, 0.0).astype(BF16)
    rank = jnp.dot(earlier, member.astype(BF16), preferred_element_type=F32)
    slot_of = seg_off + rank
    pos1 = jnp.sum(jnp.where(sel1, slot_of, 0.0), axis=-1, keepdims=True)
    pos2 = jnp.sum(jnp.where(sel2, slot_of, 0.0), axis=-1, keepdims=True)
    stats = (jnp.where(lane == 0.0, pos1, 0.0) + jnp.where(lane == 1.0, pos2, 0.0)
             + jnp.where(lane == 2.0, p1, 0.0) + jnp.where(lane == 3.0, p2, 0.0))
    pos_ref[...] = stats
    cnt_ref[0] = jnp.broadcast_to(cnt, (SUBLANES, LANES))

    st = stats.T
    slot = lax.broadcasted_iota(jnp.int32, (BLOCK_CAP, bt), 0).astype(F32)
    hit1 = slot == st[0:1, :]
    hit2 = slot == st[1:2, :]
    perm = jnp.where(hit1 | hit2, 1.0, 0.0).astype(BF16)
    hs_ref[...] = jnp.dot(perm, h.astype(BF16), preferred_element_type=F32).astype(BF16)
    wrow = jnp.where(hit1, st[2:3, :], 0.0) + jnp.where(hit2, st[3:4, :], 0.0)
    ws_ref[...] = jnp.broadcast_to(jnp.sum(wrow, axis=-1, keepdims=True), (BLOCK_CAP, LANES))


def _route(xall, g, mod, wr, *, mod_row, blocks_per_batch):
    t, d = xall.shape
    nblk = t // ROUTE_BLOCK
    return pl.pallas_call(
        functools.partial(_route_kernel, blocks_per_batch=blocks_per_batch,
                          mod_row=mod_row),
        grid=(nblk,),
        in_specs=[
            pl.BlockSpec((ROUTE_BLOCK, d), lambda i: (i, 0)),
            pl.BlockSpec((1, d), lambda i: (0, 0)),
            pl.BlockSpec((SUBLANES, N_MOD * d), lambda i: (0, 0)),
            pl.BlockSpec((d, LANES), lambda i: (0, 0)),
        ],
        out_specs=[
            pl.BlockSpec((BLOCK_CAP, d), lambda i: (i, 0)),
            pl.BlockSpec((BLOCK_CAP, LANES), lambda i: (i, 0)),
            pl.BlockSpec((1, SUBLANES, LANES), lambda i: (i, 0, 0)),
            pl.BlockSpec((ROUTE_BLOCK, LANES), lambda i: (i, 0)),
        ],
        out_shape=[
            jax.ShapeDtypeStruct((nblk * BLOCK_CAP, d), BF16),
            jax.ShapeDtypeStruct((nblk * BLOCK_CAP, LANES), F32),
            jax.ShapeDtypeStruct((nblk, SUBLANES, LANES), F32),
            jax.ShapeDtypeStruct((t, LANES), F32),
        ],
        compiler_params=pltpu.CompilerParams(
            dimension_semantics=("arbitrary",),
            vmem_limit_bytes=VMEM_LIMIT),
        name="route",
    )(xall, g.reshape(1, d), mod, wr)


def _route_tables(cnt, n_tiles_max):
    nblk = cnt.shape[0]
    n_be = (cnt[:, 0, :N_EXPERTS].astype(jnp.int32) + (CHUNK - 1)) // CHUNK
    ends_be = jnp.cumsum(n_be, axis=1)
    off_be = ends_be - n_be
    start_be = jnp.cumsum(n_be, axis=0) - n_be
    g_e = jnp.sum(n_be, axis=0)
    tiles_e = (g_e + (CHUNKS_PER_TILE - 1)) // CHUNKS_PER_TILE
    tend_e = jnp.cumsum(tiles_e)
    tstart_e = tend_e - tiles_e
    n_tiles = tend_e[-1]

    experts = jnp.arange(N_EXPERTS, dtype=jnp.int32)

    def pick(onehot, table):
        return jnp.sum(jnp.where(onehot, table, 0), axis=-1)

    c = jnp.arange(CHUNKS_PER_BLOCK, dtype=jnp.int32)
    e_bc = jnp.sum((c[None, :, None] >= ends_be[:, None, :]).astype(jnp.int32), axis=-1)
    oh_bc = e_bc[:, :, None] == experts
    shift_be = tstart_e[None, :] * CHUNKS_PER_TILE + start_be - off_be
    d_bc = pick(oh_bc, shift_be[:, None, :]) + c[None, :]
    dst = jnp.where(e_bc < N_EXPERTS, d_bc, -1).reshape(-1).astype(jnp.int32)

    ti = jnp.arange(n_tiles_max, dtype=jnp.int32)
    ti_cl = jnp.minimum(ti, n_tiles - 1)
    tile_expert = jnp.minimum(
        jnp.sum((ti_cl[:, None] >= tend_e[None, :]).astype(jnp.int32), axis=-1), N_EXPERTS - 1)
    oh_t = tile_expert[:, None] == experts
    first_q = (ti - pick(oh_t, tstart_e[None, :])) * CHUNKS_PER_TILE
    left = pick(oh_t, g_e[None, :]) - first_q
    tile_nvalid = jnp.where(ti < n_tiles, jnp.clip(left, 0, CHUNKS_PER_TILE), 0).astype(jnp.int32)

    q = first_q[:, None] + jnp.arange(CHUNKS_PER_TILE, dtype=jnp.int32)[None, :]
    cum_tb = pick(oh_t[:, None, :], (start_be + n_be)[None, :, :])
    b_tj = jnp.minimum(jnp.sum((q[:, :, None] >= cum_tb[:, None, :]).astype(jnp.int32), axis=-1),
                       nblk - 1)
    oh_b = b_tj[:, :, None] == jnp.arange(nblk, dtype=jnp.int32)
    local_tb = pick(oh_t[:, None, :], (off_be - start_be)[None, :, :])
    src = b_tj * CHUNKS_PER_BLOCK + q + pick(oh_b, local_tb[:, None, :])
    src = jnp.clip(src, 0, nblk * CHUNKS_PER_BLOCK - 1).reshape(-1).astype(jnp.int32)
    return (tile_expert.astype(jnp.int32), tile_nvalid, src,
            n_tiles.reshape(1).astype(jnp.int32), dst)


def _chunk_dmas(table_ref, base, count, sources, dst_buf, sem, n_slots, *, wait):
    def step(j, carry):
        cid = table_ref[base + j]
        live = (j < count) & (cid >= 0)
        lo = 0
        for ref, n in sources:
            @pl.when(live & (cid >= lo) & (cid < lo + n))
            def _(ref=ref, lo=lo):
                copy = pltpu.make_async_copy(
                    ref.at[pl.ds(pl.multiple_of((cid - lo) * CHUNK, CHUNK), CHUNK), :],
                    dst_buf.at[pl.ds(pl.multiple_of(j * CHUNK, CHUNK), CHUNK), :], sem)
                if wait:
                    copy.wait()
                else:
                    copy.start()
            lo += n
        return carry

    lax.fori_loop(0, n_slots, step, 0)


def _swiglu_accumulate(xb, w1_ref, w3_ref, w2_ref, acc_ref, tf):
    for c0 in range(0, tf, FF_SUB):
        c1 = min(c0 + FF_SUB, tf)
        up = jnp.dot(xb, w1_ref[:, c0:c1].astype(BF16), preferred_element_type=F32)
        lin = jnp.dot(xb, w3_ref[:, c0:c1].astype(BF16), preferred_element_type=F32)
        hid = (_silu(up) * lin).astype(BF16)
        acc_ref[...] += jnp.dot(hid, w2_ref[c0:c1, :].astype(BF16), preferred_element_type=F32)


def _moe_kernel(te_ref, nv_ref, src_ref, nt_ref, *refs, tf, src_chunks):
    n_src = len(src_chunks)
    hs_hbm = tuple(zip(refs[:n_src], src_chunks))
    ws_hbm = tuple(zip(refs[n_src:2 * n_src], src_chunks))
    w1_ref, w3_ref, w2_ref, ys_ref, xbuf, wbuf, acc_ref, sems = refs[2 * n_src:]
    i = pl.program_id(0)
    f = pl.program_id(1)
    n_tiles = nt_ref[0]
    valid = i < n_tiles
    slot = lax.rem(i, 2)

    def gather(tile, buf_slot, wait):
        base = tile * CHUNKS_PER_TILE
        _chunk_dmas(src_ref, base, nv_ref[tile], hs_hbm, xbuf.at[buf_slot], sems.at[0, buf_slot],
                    CHUNKS_PER_TILE, wait=wait)
        _chunk_dmas(src_ref, base, nv_ref[tile], ws_hbm, wbuf.at[buf_slot], sems.at[1, buf_slot],
                    CHUNKS_PER_TILE, wait=wait)

    @pl.when((i == 0) & (f == 0))
    def _():
        xbuf[...] = jnp.zeros_like(xbuf)
        wbuf[...] = jnp.zeros_like(wbuf)
        gather(0, 0, wait=False)

    @pl.when(valid & (f == 0))
    def _():
        gather(i, slot, wait=True)

        @pl.when(i + 1 < n_tiles)
        def _():
            gather(i + 1, 1 - slot, wait=False)

        acc_ref[...] = jnp.zeros_like(acc_ref)

    n_rows = nv_ref[i] * CHUNK
    for rb in range(MOE_TILE // MOE_PART):
        rows = pl.ds(rb * MOE_PART, MOE_PART)

        @pl.when(valid & (n_rows > rb * MOE_PART))
        def _(rows=rows):
            _swiglu_accumulate(xbuf[slot, rows, :], w1_ref.at[0, 0], w3_ref.at[0, 0], w2_ref.at[0, 0],
                               acc_ref.at[rows, :], tf)

    @pl.when(valid & (f == pl.num_programs(1) - 1))
    def _():
        ys_ref[...] = (acc_ref[...] * wbuf[slot][:, 0:1]).astype(BF16)

    @pl.when(jnp.logical_not(valid) & (f == 0))
    def _():
        ys_ref[...] = jnp.zeros_like(ys_ref)


def _moe_grouped(hs_list, ws_list, tables, w1, w3, w2, layer, *, n_tiles_max, tf):
    tile_expert, tile_nvalid, src, n_tiles, _ = tables
    d = hs_list[0].shape[1]
    ff = w1.shape[-1]
    nf = ff // tf
    src_chunks = tuple(h.shape[0] // CHUNK for h in hs_list)
    any_spec = pl.BlockSpec(memory_space=pl.ANY)

    def w_in_map(i, f, te, nv, sr, nt):
        return (layer, te[i], 0, jnp.where(i < nt[0], f, nf - 1))

    def w_out_map(i, f, te, nv, sr, nt):
        return (layer, te[i], jnp.where(i < nt[0], f, nf - 1), 0)

    def y_map(i, f, te, nv, sr, nt):
        return (i, 0)

    grid_spec = pltpu.PrefetchScalarGridSpec(
        num_scalar_prefetch=4,
        grid=(n_tiles_max, nf),
        in_specs=[any_spec] * (2 * len(hs_list)) + [
            pl.BlockSpec((1, 1, d, tf), w_in_map),
            pl.BlockSpec((1, 1, d, tf), w_in_map),
            pl.BlockSpec((1, 1, tf, d), w_out_map),
        ],
        out_specs=pl.BlockSpec((MOE_TILE, d), y_map),
        scratch_shapes=[
            pltpu.VMEM((2, MOE_TILE, d), BF16),
            pltpu.VMEM((2, MOE_TILE, LANES), F32),
            pltpu.VMEM((MOE_TILE, d), F32),
            pltpu.SemaphoreType.DMA((2, 2)),
        ],
    )
    return pl.pallas_call(
        functools.partial(_moe_kernel, tf=tf, src_chunks=src_chunks),
        grid_spec=grid_spec,
        out_shape=jax.ShapeDtypeStruct((n_tiles_max * MOE_TILE, d), BF16),
        compiler_params=pltpu.CompilerParams(
            dimension_semantics=("arbitrary", "arbitrary"),
            vmem_limit_bytes=VMEM_LIMIT),
        name="moe_grouped",
    )(tile_expert, tile_nvalid, src, n_tiles, *hs_list, *ws_list, w1, w3, w2)


def _combine_kernel(dst_ref, ys_hbm, pos_ref, x_ref, fg_ref, mod_ref, o_ref, ybuf, sems,
                    *, blocks_per_batch, mod_row, n_src_chunks, final):
    i = pl.program_id(0)
    row = i // blocks_per_batch if mod_row is None else mod_row
    buf = lax.rem(i, 2)

    def gather(blk, buf_slot, wait):
        _chunk_dmas(dst_ref, blk * CHUNKS_PER_BLOCK, CHUNKS_PER_BLOCK, ((ys_hbm, n_src_chunks),),
                    ybuf.at[buf_slot], sems.at[buf_slot], CHUNKS_PER_BLOCK, wait=wait)

    @pl.when(i == 0)
    def _():
        ybuf[...] = jnp.zeros_like(ybuf)
        gather(0, 0, wait=False)

    @pl.when(i + 1 < pl.num_programs(0))
    def _():
        gather(i + 1, 1 - buf, wait=False)

    stats = pos_ref[...]
    slot = lax.broadcasted_iota(jnp.int32, (stats.shape[0], BLOCK_CAP), 1).astype(F32)
    pick = jnp.where((slot == stats[:, 0:1]) | (slot == stats[:, 1:2]), 1.0, 0.0).astype(BF16)
    gather(i, buf, wait=True)
    moe = jnp.dot(pick, ybuf[buf], preferred_element_type=F32)
    y = x_ref[...] + _mod_row(mod_ref, row, 5) * moe
    if final:
        ms = jnp.mean(y * y, axis=-1, keepdims=True)
        y = y * lax.rsqrt(ms + EPS) * fg_ref[...]
    o_ref[...] = y


def _combine(ys, dst, pos, xall, fg, mod, *, mod_row, blocks_per_batch, final):
    t, d = xall.shape
    nblk = t // ROUTE_BLOCK
    grid_spec = pltpu.PrefetchScalarGridSpec(
        num_scalar_prefetch=1,
        grid=(nblk,),
        in_specs=[
            pl.BlockSpec(memory_space=pl.ANY),
            pl.BlockSpec((ROUTE_BLOCK, LANES), lambda i, ds: (i, 0)),
            pl.BlockSpec((ROUTE_BLOCK, d), lambda i, ds: (i, 0)),
            pl.BlockSpec((1, d), lambda i, ds: (0, 0)),
            pl.BlockSpec((SUBLANES, N_MOD * d), lambda i, ds: (0, 0)),
        ],
        out_specs=pl.BlockSpec((ROUTE_BLOCK, d), lambda i, ds: (i, 0)),
        scratch_shapes=[
            pltpu.VMEM((2, BLOCK_CAP, d), BF16),
            pltpu.SemaphoreType.DMA((2,)),
        ],
    )
    return pl.pallas_call(
        functools.partial(_combine_kernel, blocks_per_batch=blocks_per_batch, mod_row=mod_row,
                          n_src_chunks=ys.shape[0] // CHUNK, final=final),
        grid_spec=grid_spec,
        out_shape=jax.ShapeDtypeStruct((t, d), F32),
        compiler_params=pltpu.CompilerParams(
            dimension_semantics=("arbitrary",),
            vmem_limit_bytes=VMEM_LIMIT),
        name="moe_combine",
    )(dst, ys, pos, xall, fg.reshape(1, d), mod)


def _ffn_kernel(x_ref, g_ref, mod_ref, w1_ref, w3_ref, w2_ref, o_ref, h_ref, acc_ref,
                *, tiles_per_batch, mod_row, tf):
    i = pl.program_id(0)
    f = pl.program_id(1)
    row = i // tiles_per_batch if mod_row is None else mod_row

    @pl.when(f == 0)
    def _():
        h = _rms_mod(x_ref[...], g_ref[...], _mod_row(mod_ref, row, 3), _mod_row(mod_ref, row, 4))
        h_ref[...] = h.astype(BF16)
        acc_ref[...] = jnp.zeros_like(acc_ref)

    _swiglu_accumulate(h_ref[...], w1_ref.at[0], w3_ref.at[0], w2_ref.at[0], acc_ref, tf)

    @pl.when(f == pl.num_programs(1) - 1)
    def _():
        o_ref[...] = x_ref[...] + _mod_row(mod_ref, row, 5) * acc_ref[...]


def _ffn(x2, g, mod, w1, w3, w2, layer, *, tm, tf, tiles_per_batch, mod_row):
    t, d = x2.shape
    ff = w1.shape[-1]
    return pl.pallas_call(
        functools.partial(_ffn_kernel, tiles_per_batch=tiles_per_batch, mod_row=mod_row, tf=tf),
        grid=(t // tm, ff // tf),
        in_specs=[
            pl.BlockSpec((tm, d), lambda i, f: (i, 0)),
            pl.BlockSpec((1, d), lambda i, f: (0, 0)),
            pl.BlockSpec((SUBLANES, N_MOD * d), lambda i, f: (0, 0)),
            pl.BlockSpec((1, d, tf), lambda i, f: (layer, 0, f)),
            pl.BlockSpec((1, d, tf), lambda i, f: (layer, 0, f)),
            pl.BlockSpec((1, tf, d), lambda i, f: (layer, f, 0)),
        ],
        out_specs=pl.BlockSpec((tm, d), lambda i, f: (i, 0)),
        out_shape=jax.ShapeDtypeStruct((t, d), F32),
        scratch_shapes=[pltpu.VMEM((tm, d), BF16), pltpu.VMEM((tm, d), F32)],
        compiler_params=pltpu.CompilerParams(
            dimension_semantics=("arbitrary", "arbitrary"),
            vmem_limit_bytes=VMEM_LIMIT),
        name="dense_ffn",
    )(x2, g.reshape(1, d), mod, w1, w3, w2)


def _sincos_2d(rows, cols, dim):
    quarter = dim // 4
    omega = 1.0 / (10000.0 ** (jnp.arange(quarter, dtype=F32) / quarter))

    def emb1d(n):
        ang = jnp.arange(n, dtype=F32)[:, None] * omega[None, :]
        return jnp.concatenate([jnp.sin(ang), jnp.cos(ang)], axis=-1)

    er = jnp.broadcast_to(emb1d(rows)[:, None, :], (rows, cols, dim // 2))
    ec = jnp.broadcast_to(emb1d(cols)[None, :, :], (rows, cols, dim // 2))
    return jnp.concatenate([er, ec], axis=-1).reshape(rows * cols, dim)


def _block_diag_tiles(w):
    per = MXU_DIM // LRU_HEAD_DIM
    w4 = w.reshape(D_LRU // MXU_DIM, per, LRU_HEAD_DIM, LRU_HEAD_DIM)
    eye = jnp.eye(per, dtype=w.dtype)
    return jnp.einsum("ghij,hk->ghikj", w4, eye).reshape(D_LRU // MXU_DIM, MXU_DIM, MXU_DIM)


def _gate_weights(wa, wx):
    return jnp.concatenate([_block_diag_tiles(wa), _block_diag_tiles(wx)], axis=-1).astype(BF16)


def _lru_vectors(conv_w, conv_b, ba, bx, lam):
    rows = jnp.concatenate([conv_w, conv_b[None], ba[None], bx[None], lam[None]], axis=0)
    return jnp.pad(rows, ((0, 2 * SUBLANES - rows.shape[0]), (0, 0)))


def kernel(x, c, ctx, c_ctx, w_mod, b_mod, norm1_g, norm2_g, w_in, w_out, lru_conv_w, lru_conv_b, lru_wa, lru_ba, lru_wx, lru_bx, lru_lambda, conf_dw_w, conf_dw_b, conf_ln_g, conf_ln_b, ffn_w1, ffn_w3, ffn_w2, moe_router, moe_w1, moe_w3, moe_w2, final_g):
    nb, s, d = x.shape
    n_ctx = ctx.shape[1]
    pos = _sincos_2d(s // GRID_W, GRID_W, d)
    cc = jnp.concatenate([c, c_ctx[None], jnp.zeros((SUBLANES - nb - 1, d), F32)], axis=0)
    mod_all = _modulation(cc, w_mod, b_mod)

    ts_x, ts_c = 512, n_ctx
    tm_x = 1024
    tiles_per_batch = s // tm_x
    zero_state = jnp.zeros((nb, SUBLANES, D_LRU), F32)
    xc = ctx.reshape(1, nb * n_ctx, d)

    ffn_w1_b, ffn_w3_b, ffn_w2_b = (w.astype(BF16) for w in (ffn_w1, ffn_w3, ffn_w2))

    for l in range(DEPTH):
        last = l == DEPTH - 1
        mod = mod_all[l]
        w_in_b = w_in[l].astype(BF16)
        w_out_b = w_out[l].astype(BF16)
        pvec = [_lru_vectors(lru_conv_w[l], lru_conv_b[l], lru_ba[l, dr], lru_bx[l, dr],
                             lru_lambda[l, dr]) for dr in range(2)]
        wg = [_gate_weights(lru_wa[l, dr], lru_wx[l, dr]) for dr in range(2)]
        dww = jnp.repeat(conf_dw_w[l], SUBLANES, axis=0)
        cvec = jnp.pad(jnp.stack([conf_dw_b[l], conf_ln_g[l], conf_ln_b[l]]),
                       ((0, SUBLANES - 3), (0, 0)))

        u_x, x = _in_proj(x, pos if l == 0 else None, norm1_g[l], mod, w_in_b,
                          tm=tm_x, mod_row=None)
        u_c, _ = _in_proj(xc, None, norm1_g[l], mod, w_in_b[:, :D_LRU] if last else w_in_b,
                          tm=512, mod_row=CTX_MOD_ROW)
        u_c = u_c.reshape(nb, n_ctx, u_c.shape[-1])
        hf_c, rc_c, st_f = _lru_forward(u_c, pvec[0], wg[0], zero_state, ts=ts_c)
        hf_x, rc_x, _ = _lru_forward(u_x, pvec[0], wg[0], st_f, ts=ts_x)
        if last:
            _, st_b = _lru_backward(rc_c, pvec[1], wg[1], zero_state, ts=ts_c)
        else:
            xc_new, st_b = _lru_backward(
                rc_c, pvec[1], wg[1], zero_state, ts=ts_c,
                merge_args=(u_c, hf_c, xc.reshape(nb, n_ctx, d), mod, dww, cvec, w_out_b),
                mod_row=CTX_MOD_ROW)
            xc = xc_new.reshape(1, nb * n_ctx, d)
        x, _ = _lru_backward(rc_x, pvec[1], wg[1], st_b, ts=ts_x,
                             merge_args=(u_x, hf_x, x, mod, dww, cvec, w_out_b), mod_row=None)

        j = l // 2
        x2 = x.reshape(nb * s, d)
        if l % 2 == 0:
            x2 = _ffn(x2, norm2_g[l], mod, ffn_w1_b, ffn_w3_b, ffn_w2_b, j, tm=tm_x, tf=FF_BLOCK,
                      tiles_per_batch=tiles_per_batch, mod_row=None)
            xc2 = _ffn(xc[0], norm2_g[l], mod, ffn_w1_b, ffn_w3_b, ffn_w2_b, j, tm=512, tf=FF_BLOCK,
                       tiles_per_batch=1, mod_row=CTX_MOD_ROW)
            xc = xc2.reshape(1, nb * n_ctx, d)
        else:
            groups = [(x2, None)] if last else [(x2, None), (xc[0], CTX_MOD_ROW)]
            blocks_per_batch = s // ROUTE_BLOCK
            wr = jnp.pad(moe_router[j], ((0, 0), (0, LANES - N_EXPERTS)))
            routed = [_route(xg, norm2_g[l], mod, wr, mod_row=row, blocks_per_batch=blocks_per_batch)
                      for xg, row in groups]
            cnt = jnp.concatenate([r[2] for r in routed], axis=0)
            nblk = cnt.shape[0]
            n_tiles_max = (nblk * CHUNKS_PER_BLOCK + N_EXPERTS * (CHUNKS_PER_TILE - 1)) // CHUNKS_PER_TILE
            tables = _route_tables(cnt, n_tiles_max)
            ys = _moe_grouped([r[0] for r in routed], [r[1] for r in routed], tables,
                              moe_w1, moe_w3, moe_w2, j, n_tiles_max=n_tiles_max, tf=FF_BLOCK)
            outs, first = [], 0
            for (xg, row), r in zip(groups, routed):
                n_chunks = xg.shape[0] // ROUTE_BLOCK * CHUNKS_PER_BLOCK
                outs.append(_combine(ys, tables[4][first:first + n_chunks], r[3], xg, final_g, mod,
                                     mod_row=row, blocks_per_batch=blocks_per_batch, final=last))
                first += n_chunks
            x2 = outs[0]
            if not last:
                xc = outs[1].reshape(1, nb * n_ctx, d)
        x = x2.reshape(nb, s, d)
    return x
```

```python
import functools
import math

import jax
import jax.numpy as jnp
from jax import lax
from jax.experimental import pallas as pl
from jax.experimental.pallas import tpu as pltpu

F32 = jnp.float32
BF16 = jnp.bfloat16

D_MODEL = 1024
DEPTH = 4
GRID_W = 64
D_LRU = 512
D_CONV = 512
LRU_HEADS = 8
LRU_HEAD_DIM = 64
LRU_CONV_W = 4
LRU_C = 8.0
CONF_CONV_W = 31
D_FF = 2816
N_EXPERTS = 8
N_MOD = 6
EPS = 1e-6

LANES = 128
SUBLANES = 8
BF16_ROWS = 16
MXU_DIM = 256
VMEM_LIMIT = 60 * 1024 * 1024

HALO = BF16_ROWS
CONV_CHUNK = 32
CTX_MOD_ROW = 4


def _sigmoid(v):
    return 0.5 * jnp.tanh(0.5 * v) + 0.5


def _silu(v):
    h = 0.5 * v
    return h + h * jnp.tanh(h)


def _rms_mod(x, g, shift, scale):
    ms = jnp.mean(x * x, axis=-1, keepdims=True)
    y = x * lax.rsqrt(ms + EPS) * g
    return y * (1.0 + scale) + shift


def _mod_row(mod_ref, row, k):
    r = mod_ref[pl.ds(row, 1), :]
    return r[:, k * D_MODEL:(k + 1) * D_MODEL]


def _mod_kernel(cc_ref, w_ref, b_ref, o_ref):
    s = _silu(cc_ref[...])
    o_ref[0] = jnp.dot(s.astype(BF16), w_ref[0].astype(BF16),
                       preferred_element_type=F32) + b_ref[0]


def _modulation(cc, w_mod, b_mod):
    n_out = N_MOD * D_MODEL
    tn = 1536
    return pl.pallas_call(
        _mod_kernel,
        grid=(DEPTH, n_out // tn),
        in_specs=[
            pl.BlockSpec((SUBLANES, D_MODEL), lambda l, j: (0, 0)),
            pl.BlockSpec((1, D_MODEL, tn), lambda l, j: (l, 0, j)),
            pl.BlockSpec((1, 1, tn), lambda l, j: (l, 0, j)),
        ],
        out_specs=pl.BlockSpec((1, SUBLANES, tn), lambda l, j: (l, 0, j)),
        out_shape=jax.ShapeDtypeStruct((DEPTH, SUBLANES, n_out), F32),
        compiler_params=pltpu.CompilerParams(
            dimension_semantics=("arbitrary", "arbitrary"),
            vmem_limit_bytes=VMEM_LIMIT),
        name="modulation",
    )(cc, w_mod, b_mod.reshape(DEPTH, 1, n_out))


def _short_conv(prev, main, nxt, pvec_ref, ts):
    groups = ts // SUBLANES
    ext = jnp.concatenate([prev, main, nxt], axis=0)
    x3 = ext.reshape(groups + 2, SUBLANES, D_LRU)
    row = lax.broadcasted_iota(jnp.int32, (groups, SUBLANES, D_LRU), 1)

    def window(off):
        if off == 0:
            return x3[1:groups + 1]
        r = pltpu.roll(x3, (-off) % SUBLANES, axis=1)
        if off < 0:
            return jnp.where(row >= -off, r[1:groups + 1], r[0:groups])
        return jnp.where(row < SUBLANES - off, r[1:groups + 1], r[2:groups + 2])

    rc = jnp.zeros((groups, SUBLANES, D_LRU), F32) + pvec_ref[4:5, :]
    for k in range(LRU_CONV_W):
        rc = rc + pvec_ref[k:k + 1, :] * window(k - 2)
    return rc.reshape(ts, D_LRU)


def _lru_gates(rc, pvec_ref, wg_ref):
    rcb = rc.astype(BF16)
    ga, gx = [], []
    for j in range(D_LRU // MXU_DIM):
        o = jnp.dot(rcb[:, j * MXU_DIM:(j + 1) * MXU_DIM], wg_ref[j],
                    preferred_element_type=F32)
        ga.append(o[:, :MXU_DIM])
        gx.append(o[:, MXU_DIM:])
    r_gate = _sigmoid(jnp.concatenate(ga, axis=-1) + pvec_ref[5:6, :])
    i_gate = _sigmoid(jnp.concatenate(gx, axis=-1) + pvec_ref[6:7, :])
    neg_lam = -pvec_ref[7:8, :]
    softplus = jnp.maximum(neg_lam, 0.0) + jnp.log1p(jnp.exp(-jnp.abs(neg_lam)))
    log_a = -LRU_C * r_gate * softplus
    a = jnp.exp(log_a)
    one_minus_a2 = -jnp.tanh(log_a) * (a * a + 1.0)
    b = jnp.sqrt(one_minus_a2) * (i_gate * rc)
    return a, b


def _lru_scan(a, b, h_in, hs_ref, ts, reverse):
    groups = ts // SUBLANES
    c = a.shape[-1]
    a3 = a.reshape(groups, SUBLANES, c)
    b3 = b.reshape(groups, SUBLANES, c)
    row = lax.broadcasted_iota(jnp.int32, (groups, SUBLANES, c), 1)
    for s in (1, 2, 4):
        shift = SUBLANES - s if reverse else s
        a_sh = pltpu.roll(a3, shift, axis=1)
        b_sh = pltpu.roll(b3, shift, axis=1)
        m = (row < SUBLANES - s) if reverse else (row >= s)
        b3 = jnp.where(m, a3 * b_sh + b3, b3)
        a3 = jnp.where(m, a3 * a_sh, a3)
    h = h_in
    order = range(groups - 1, -1, -1) if reverse else range(groups)
    edge = 0 if reverse else SUBLANES - 1
    for g in order:
        hg = a3[g] * h + b3[g]
        hs_ref[g * SUBLANES:(g + 1) * SUBLANES, :] = hg
        h = jnp.broadcast_to(hg[edge:edge + 1, :], (SUBLANES, c))
    return h


def _halo_specs(width, col, ts, s, tile_of):
    per = ts // HALO
    last = s // HALO - 1
    main = pl.BlockSpec((1, ts, width), lambda b, t: (b, tile_of(t), col))
    prev = pl.BlockSpec((1, HALO, width),
                        lambda b, t: (b, jnp.maximum(tile_of(t) * per - 1, 0), col))
    nxt = pl.BlockSpec((1, HALO, width),
                       lambda b, t: (b, jnp.minimum((tile_of(t) + 1) * per, last), col))
    return [main, prev, nxt]


def _in_fwd_kernel(*refs, ts, add_pos, mod_row):
    if add_pos:
        (x_ref, xp_ref, xn_ref, pos_ref, posp_ref, posn_ref, g_ref, mod_ref, w_ref, pvec_ref, wg_ref,
         seed_ref, u_ref, hf_ref, rc_ref, state_ref, xo_ref, hs_ref, carry_ref) = refs
    else:
        (x_ref, xp_ref, xn_ref, g_ref, mod_ref, w_ref, pvec_ref, wg_ref,
         seed_ref, u_ref, hf_ref, rc_ref, state_ref, hs_ref, carry_ref) = refs
    t = pl.program_id(1)
    nt = pl.num_programs(1)
    row = pl.program_id(0) if mod_row is None else mod_row

    @pl.when(t == 0)
    def _():
        carry_ref[...] = seed_ref[0]

    g = g_ref[...]
    shift = _mod_row(mod_ref, row, 0)
    scale = _mod_row(mod_ref, row, 1)
    x, xp, xn = x_ref[0], xp_ref[0], xn_ref[0]
    if add_pos:
        x, xp, xn = x + pos_ref[...], xp + posp_ref[...], xn + posn_ref[...]
        xo_ref[0] = x
    u = jnp.dot(_rms_mod(x, g, shift, scale).astype(BF16), w_ref[...], preferred_element_type=F32)
    u_ref[0] = u.astype(BF16)

    def halo_rows(xh, keep):
        h = _rms_mod(xh, g, shift, scale).astype(BF16)
        return jnp.dot(h, w_ref[:, 0:D_LRU], preferred_element_type=F32) * keep

    prev = halo_rows(xp, (t > 0).astype(F32))[HALO - SUBLANES:HALO, :]
    nxt = halo_rows(xn, (t < nt - 1).astype(F32))[0:SUBLANES, :]
    rc = _short_conv(prev, u[:, 0:D_LRU], nxt, pvec_ref, ts)
    rc_ref[0] = rc
    a, b = _lru_gates(rc, pvec_ref, wg_ref)
    h = _lru_scan(a, b, carry_ref[...], hs_ref, ts, reverse=False)
    carry_ref[...] = h
    state_ref[0] = h
    hf_ref[0] = hs_ref[...].astype(BF16)


def _in_forward(x, pos, g, mod, w, pvec, wg, seed, *, ts, mod_row):
    nb, s, d = x.shape
    n = w.shape[1]
    add_pos = pos is not None
    per = ts // HALO
    last = s // HALO - 1

    def prev_blk(t):
        return jnp.maximum(t * per - 1, 0)

    def next_blk(t):
        return jnp.minimum((t + 1) * per, last)

    in_specs = [
        pl.BlockSpec((1, ts, d), lambda b, t: (b, t, 0)),
        pl.BlockSpec((1, HALO, d), lambda b, t: (b, prev_blk(t), 0)),
        pl.BlockSpec((1, HALO, d), lambda b, t: (b, next_blk(t), 0)),
    ]
    args = [x, x, x]
    if add_pos:
        in_specs += [
            pl.BlockSpec((ts, d), lambda b, t: (t, 0)),
            pl.BlockSpec((HALO, d), lambda b, t: (prev_blk(t), 0)),
            pl.BlockSpec((HALO, d), lambda b, t: (next_blk(t), 0)),
        ]
        args += [pos, pos, pos]
    in_specs += [
        pl.BlockSpec((1, d), lambda b, t: (0, 0)),
        pl.BlockSpec((SUBLANES, N_MOD * d), lambda b, t: (0, 0)),
        pl.BlockSpec((d, n), lambda b, t: (0, 0)),
        pl.BlockSpec((2 * SUBLANES, D_LRU), lambda b, t: (0, 0)),
        pl.BlockSpec((D_LRU // MXU_DIM, MXU_DIM, 2 * MXU_DIM), lambda b, t: (0, 0, 0)),
        pl.BlockSpec((1, SUBLANES, D_LRU), lambda b, t: (b, 0, 0)),
    ]
    args += [g.reshape(1, d), mod, w, pvec, wg, seed]
    out_specs = [
        pl.BlockSpec((1, ts, n), lambda b, t: (b, t, 0)),
        pl.BlockSpec((1, ts, D_LRU), lambda b, t: (b, t, 0)),
        pl.BlockSpec((1, ts, D_LRU), lambda b, t: (b, t, 0)),
        pl.BlockSpec((1, SUBLANES, D_LRU), lambda b, t: (b, 0, 0)),
    ]
    out_shape = [
        jax.ShapeDtypeStruct((nb, s, n), BF16),
        jax.ShapeDtypeStruct((nb, s, D_LRU), BF16),
        jax.ShapeDtypeStruct((nb, s, D_LRU), F32),
        jax.ShapeDtypeStruct((nb, SUBLANES, D_LRU), F32),
    ]
    if add_pos:
        out_specs.append(pl.BlockSpec((1, ts, d), lambda b, t: (b, t, 0)))
        out_shape.append(jax.ShapeDtypeStruct((nb, s, d), F32))
    res = pl.pallas_call(
        functools.partial(_in_fwd_kernel, ts=ts, add_pos=add_pos, mod_row=mod_row),
        grid=(nb, s // ts),
        in_specs=in_specs,
        out_specs=out_specs,
        out_shape=out_shape,
        scratch_shapes=[
            pltpu.VMEM((ts, D_LRU), F32),
            pltpu.VMEM((SUBLANES, D_LRU), F32),
        ],
        compiler_params=pltpu.CompilerParams(
            dimension_semantics=("arbitrary", "arbitrary"),
            vmem_limit_bytes=VMEM_LIMIT),
        name="in_forward",
    )(*args)
    u, hf, rc, state = res[:4]
    return u, hf, rc, state, (res[4] if add_pos else x)


def _bwd_kernel(*refs, ts, merge, mod_row):
    if merge:
        (rc_ref, pvec_ref, wg_ref, seed_ref,
         gate_ref, v_ref, vprev_ref, vnext_ref, hf_ref, x_ref, mod_ref,
         dww_ref, cvec_ref, wo_ref,
         xo_ref, state_ref,
         hs_ref, carry_ref, g_ref, cs_ref, sh_ref) = refs
    else:
        (rc_ref, pvec_ref, wg_ref, seed_ref,
         state_ref, hs_ref, carry_ref) = refs
    tg = pl.program_id(1)
    nt = pl.num_programs(1)
    t = nt - 1 - tg

    @pl.when(tg == 0)
    def _():
        carry_ref[...] = seed_ref[0]

    has_prev = (t > 0).astype(F32)
    has_next = (t < nt - 1).astype(F32)
    a, b = _lru_gates(rc_ref[0], pvec_ref, wg_ref)
    h = _lru_scan(a, b, carry_ref[...], hs_ref, ts, reverse=True)
    carry_ref[...] = h
    state_ref[0] = h
    if not merge:
        return

    y = hf_ref[0].astype(F32) + hs_ref[...]
    gate = gate_ref[0].astype(F32)
    cdf = 0.5 * (1.0 + jnp.tanh(math.sqrt(2.0 / math.pi) * (gate + 0.044715 * (gate * gate * gate))))
    lru = (y * (gate * cdf)).astype(BF16)

    def glu(vref):
        v = vref[0].astype(F32)
        return v[:, :D_CONV] * _sigmoid(v[:, D_CONV:])

    g_ref[0:HALO, :] = glu(vprev_ref) * has_prev
    g_ref[HALO:HALO + ts, :] = glu(v_ref)
    g_ref[HALO + ts:2 * HALO + ts, :] = glu(vnext_ref) * has_next
    pad = (CONF_CONV_W - 1) // 2
    dwb = cvec_ref[0:1, :]
    span = ts + 2 * HALO - SUBLANES
    for r in range(1, SUBLANES):
        sh_ref[r - 1, 0:span, :] = g_ref[pl.ds(r, span), :]

    def conv_chunk(i, carry):
        base = pl.multiple_of(i * CONV_CHUNK, CONV_CHUNK)
        groups = CONV_CHUNK // SUBLANES
        acc = jnp.zeros((groups, SUBLANES, D_CONV), F32) + dwb
        for k in range(CONF_CONV_W):
            q, r = divmod(HALO - pad + k, SUBLANES)
            rows = pl.ds(base + q * SUBLANES, CONV_CHUNK)
            tap = g_ref[rows, :] if r == 0 else sh_ref[r - 1, rows, :]
            wk = dww_ref[k * SUBLANES:(k + 1) * SUBLANES, :]
            acc = acc + wk[None] * tap.reshape(groups, SUBLANES, D_CONV)
        cs_ref[pl.ds(base, CONV_CHUNK), :] = acc.reshape(CONV_CHUNK, D_CONV)
        return carry

    lax.fori_loop(0, ts // CONV_CHUNK, conv_chunk, 0)
    cv = cs_ref[...]
    mu = jnp.mean(cv, axis=-1, keepdims=True)
    var = jnp.mean(jnp.square(cv - mu), axis=-1, keepdims=True)
    ln = (cv - mu) * lax.rsqrt(var + EPS) * cvec_ref[1:2, :] + cvec_ref[2:3, :]
    conf = _silu(ln).astype(BF16)

    o = jnp.dot(lru, wo_ref[0:D_LRU, :], preferred_element_type=F32)
    o = o + jnp.dot(conf, wo_ref[D_LRU:, :], preferred_element_type=F32)
    row = pl.program_id(0) if mod_row is None else mod_row
    xo_ref[0] = x_ref[0] + _mod_row(mod_ref, row, 2) * o


def _lru_backward(rc, pvec, wg, seed, *, ts, merge_args=None, mod_row=None):
    nb, s, _ = rc.shape
    nt = s // ts
    rev = lambda t: nt - 1 - t
    in_specs = [
        pl.BlockSpec((1, ts, D_LRU), lambda b, t: (b, rev(t), 0)),
        pl.BlockSpec((2 * SUBLANES, D_LRU), lambda b, t: (0, 0)),
        pl.BlockSpec((D_LRU // MXU_DIM, MXU_DIM, 2 * MXU_DIM), lambda b, t: (0, 0, 0)),
        pl.BlockSpec((1, SUBLANES, D_LRU), lambda b, t: (b, 0, 0)),
    ]
    args = [rc, pvec, wg, seed]
    state_spec = pl.BlockSpec((1, SUBLANES, D_LRU), lambda b, t: (b, 0, 0))
    state_shape = jax.ShapeDtypeStruct((nb, SUBLANES, D_LRU), F32)
    scratch = [
        pltpu.VMEM((ts, D_LRU), F32),
        pltpu.VMEM((SUBLANES, D_LRU), F32),
    ]
    merge = merge_args is not None
    if merge:
        u, hf, x, mod, dww, cvec, wo = merge_args
        d = x.shape[-1]
        in_specs += [pl.BlockSpec((1, ts, D_LRU), lambda b, t: (b, rev(t), 1))]
        in_specs += _halo_specs(2 * D_CONV, 1, ts, s, rev)
        in_specs += [
            pl.BlockSpec((1, ts, D_LRU), lambda b, t: (b, rev(t), 0)),
            pl.BlockSpec((1, ts, d), lambda b, t: (b, rev(t), 0)),
            pl.BlockSpec((SUBLANES, N_MOD * d), lambda b, t: (0, 0)),
            pl.BlockSpec((CONF_CONV_W * SUBLANES, D_CONV), lambda b, t: (0, 0)),
            pl.BlockSpec((SUBLANES, D_CONV), lambda b, t: (0, 0)),
            pl.BlockSpec((d, d), lambda b, t: (0, 0)),
        ]
        args += [u, u, u, u, hf, x, mod, dww, cvec, wo]
        out_specs = [pl.BlockSpec((1, ts, d), lambda b, t: (b, rev(t), 0)), state_spec]
        out_shape = [jax.ShapeDtypeStruct((nb, s, d), F32), state_shape]
        scratch += [
            pltpu.VMEM((ts + 2 * HALO, D_CONV), F32),
            pltpu.VMEM((ts, D_CONV), F32),
            pltpu.VMEM((SUBLANES - 1, ts + 2 * HALO, D_CONV), F32),
        ]
    else:
        out_specs = [state_spec]
        out_shape = [state_shape]
    res = pl.pallas_call(
        functools.partial(_bwd_kernel, ts=ts, merge=merge, mod_row=mod_row),
        grid=(nb, nt),
        in_specs=in_specs,
        out_specs=out_specs,
        out_shape=out_shape,
        scratch_shapes=scratch,
        compiler_params=pltpu.CompilerParams(
            dimension_semantics=("arbitrary", "arbitrary"),
            vmem_limit_bytes=VMEM_LIMIT),
        name="lru_backward_merge" if merge else "lru_backward",
    )(*args)
    return res if merge else (None, res[0])


ROUTE_BLOCK = 512
CHUNK = BF16_ROWS
BLOCK_CAP = 2 * ROUTE_BLOCK + LANES
CHUNKS_PER_BLOCK = BLOCK_CAP // CHUNK
MOE_TILE = 1024
FF_BLOCK = D_FF // 2
FF_SUB = MXU_DIM
CHUNKS_PER_TILE = MOE_TILE // CHUNK


def _route_kernel(x_ref, g_ref, mod_ref, wr_ref, hs_ref, ws_ref, cnt_ref, pos_ref,
                  *, blocks_per_batch, mod_row):
    i = pl.program_id(0)
    row = i // blocks_per_batch if mod_row is None else mod_row
    h = _rms_mod(x_ref[...], g_ref[...], _mod_row(mod_ref, row, 3), _mod_row(mod_ref, row, 4))
    wr = wr_ref[...]
    h_hi = h.astype(BF16)
    h_lo = (h - h_hi.astype(F32)).astype(BF16)
    w_hi = wr.astype(BF16)
    w_lo = (wr - w_hi.astype(F32)).astype(BF16)
    logits = (jnp.dot(h_hi, w_hi, preferred_element_type=F32)
              + jnp.dot(h_hi, w_lo, preferred_element_type=F32)
              + jnp.dot(h_lo, w_hi, preferred_element_type=F32))
    lane = lax.broadcasted_iota(jnp.int32, logits.shape, 1).astype(F32)
    neg = jnp.float32(-jnp.inf)
    l1 = jnp.where(lane < N_EXPERTS, logits, neg)
    m1 = jnp.max(l1, axis=-1, keepdims=True)
    i1 = jnp.min(jnp.where(l1 == m1, lane, float(LANES)), axis=-1, keepdims=True)
    l2 = jnp.where(lane == i1, neg, l1)
    m2 = jnp.max(l2, axis=-1, keepdims=True)
    i2 = jnp.min(jnp.where(l2 == m2, lane, float(LANES)), axis=-1, keepdims=True)
    e = jnp.exp(m2 - m1)
    p1 = 1.0 / (1.0 + e)
    p2 = e / (1.0 + e)

    sel1 = lane == i1
    sel2 = lane == i2
    member = jnp.where(sel1 | sel2, 1.0, 0.0)
    cnt = jnp.sum(member, axis=0, keepdims=True)
    padded = jnp.floor((cnt + (CHUNK - 1.0)) * (1.0 / CHUNK)) * CHUNK
    r128 = lax.broadcasted_iota(jnp.int32, (LANES, LANES), 0)
    c128 = lax.broadcasted_iota(jnp.int32, (LANES, LANES), 1)
    before = jnp.where(r128 < c128, 1.0, 0.0)
    seg_off = jnp.dot(jnp.broadcast_to(padded, (SUBLANES, LANES)), before,
                      preferred_element_type=F32, precision=lax.Precision.HIGHEST)[0:1, :]
    bt = member.shape[0]
    rt = lax.broadcasted_iota(jnp.int32, (bt, bt), 0)
    ct = lax.broadcasted_iota(jnp.int32, (bt, bt), 1)
    earlier = jnp.where(ct < rt, 1.0, 0.0).astype(BF16)
    rank = jnp.dot(earlier, member.astype(BF16), preferred_element_type=F32)
    slot_of = seg_off + rank
    pos1 = jnp.sum(jnp.where(sel1, slot_of, 0.0), axis=-1, keepdims=True)
    pos2 = jnp.sum(jnp.where(sel2, slot_of, 0.0), axis=-1, keepdims=True)
    stats = (jnp.where(lane == 0.0, pos1, 0.0) + jnp.where(lane == 1.0, pos2, 0.0)
             + jnp.where(lane == 2.0, p1, 0.0) + jnp.where(lane == 3.0, p2, 0.0))
    pos_ref[...] = stats
    cnt_ref[0] = jnp.broadcast_to(cnt, (SUBLANES, LANES))

    st = stats.T
    slot = lax.broadcasted_iota(jnp.int32, (BLOCK_CAP, bt), 0).astype(F32)
    hit1 = slot == st[0:1, :]
    hit2 = slot == st[1:2, :]
    perm = jnp.where(hit1 | hit2, 1.0, 0.0).astype(BF16)
    hs_ref[...] = jnp.dot(perm, h.astype(BF16), preferred_element_type=F32).astype(BF16)
    wrow = jnp.where(hit1, st[2:3, :], 0.0) + jnp.where(hit2, st[3:4, :], 0.0)
    ws_ref[...] = jnp.broadcast_to(jnp.sum(wrow, axis=-1, keepdims=True), (BLOCK_CAP, LANES))


def _route(xall, g, mod, wr, *, mod_row, blocks_per_batch):
    t, d = xall.shape
    nblk = t // ROUTE_BLOCK
    return pl.pallas_call(
        functools.partial(_route_kernel, blocks_per_batch=blocks_per_batch,
                          mod_row=mod_row),
        grid=(nblk,),
        in_specs=[
            pl.BlockSpec((ROUTE_BLOCK, d), lambda i: (i, 0)),
            pl.BlockSpec((1, d), lambda i: (0, 0)),
            pl.BlockSpec((SUBLANES, N_MOD * d), lambda i: (0, 0)),
            pl.BlockSpec((d, LANES), lambda i: (0, 0)),
        ],
        out_specs=[
            pl.BlockSpec((BLOCK_CAP, d), lambda i: (i, 0)),
            pl.BlockSpec((BLOCK_CAP, LANES), lambda i: (i, 0)),
            pl.BlockSpec((1, SUBLANES, LANES), lambda i: (i, 0, 0)),
            pl.BlockSpec((ROUTE_BLOCK, LANES), lambda i: (i, 0)),
        ],
        out_shape=[
            jax.ShapeDtypeStruct((nblk * BLOCK_CAP, d), BF16),
            jax.ShapeDtypeStruct((nblk * BLOCK_CAP, LANES), F32),
            jax.ShapeDtypeStruct((nblk, SUBLANES, LANES), F32),
            jax.ShapeDtypeStruct((t, LANES), F32),
        ],
        compiler_params=pltpu.CompilerParams(
            dimension_semantics=("arbitrary",),
            vmem_limit_bytes=VMEM_LIMIT),
        name="route",
    )(xall, g.reshape(1, d), mod, wr)


def _route_tables(cnt, n_tiles_max):
    nblk = cnt.shape[0]
    n_be = (cnt[:, 0, :N_EXPERTS].astype(jnp.int32) + (CHUNK - 1)) // CHUNK
    ends_be = jnp.cumsum(n_be, axis=1)
    off_be = ends_be - n_be
    start_be = jnp.cumsum(n_be, axis=0) - n_be
    g_e = jnp.sum(n_be, axis=0)
    tiles_e = (g_e + (CHUNKS_PER_TILE - 1)) // CHUNKS_PER_TILE
    tend_e = jnp.cumsum(tiles_e)
    tstart_e = tend_e - tiles_e
    n_tiles = tend_e[-1]

    experts = jnp.arange(N_EXPERTS, dtype=jnp.int32)

    def pick(onehot, table):
        return jnp.sum(jnp.where(onehot, table, 0), axis=-1)

    c = jnp.arange(CHUNKS_PER_BLOCK, dtype=jnp.int32)
    e_bc = jnp.sum((c[None, :, None] >= ends_be[:, None, :]).astype(jnp.int32), axis=-1)
    oh_bc = e_bc[:, :, None] == experts
    shift_be = tstart_e[None, :] * CHUNKS_PER_TILE + start_be - off_be
    d_bc = pick(oh_bc, shift_be[:, None, :]) + c[None, :]
    dst = jnp.where(e_bc < N_EXPERTS, d_bc, -1).reshape(-1).astype(jnp.int32)

    ti = jnp.arange(n_tiles_max, dtype=jnp.int32)
    ti_cl = jnp.minimum(ti, n_tiles - 1)
    tile_expert = jnp.minimum(
        jnp.sum((ti_cl[:, None] >= tend_e[None, :]).astype(jnp.int32), axis=-1), N_EXPERTS - 1)
    oh_t = tile_expert[:, None] == experts
    first_q = (ti - pick(oh_t, tstart_e[None, :])) * CHUNKS_PER_TILE

    q = first_q[:, None] + jnp.arange(CHUNKS_PER_TILE, dtype=jnp.int32)[None, :]
    cum_tb = pick(oh_t[:, None, :], (start_be + n_be)[None, :, :])
    b_tj = jnp.minimum(jnp.sum((q[:, :, None] >= cum_tb[:, None, :]).astype(jnp.int32), axis=-1),
                       nblk - 1)
    oh_b = b_tj[:, :, None] == jnp.arange(nblk, dtype=jnp.int32)
    local_tb = pick(oh_t[:, None, :], (off_be - start_be)[None, :, :])
    src = b_tj * CHUNKS_PER_BLOCK + q + pick(oh_b, local_tb[:, None, :])
    src = jnp.clip(src, 0, nblk * CHUNKS_PER_BLOCK - 1).reshape(-1).astype(jnp.int32)
    return tile_expert.astype(jnp.int32), src, n_tiles.reshape(1).astype(jnp.int32), dst


DMA_UNROLL = 8


def _start_chunk_gather(table_ref, base, sources, dst_buf, sem):
    n_slots = dst_buf.shape[0] // CHUNK

    def step(j, carry):
        cid = jnp.maximum(table_ref[base + j], 0)
        dst = dst_buf.at[pl.ds(pl.multiple_of(j * CHUNK, CHUNK), CHUNK), :]
        lo = 0
        for ref, n in sources:
            def start(ref=ref, lo=lo):
                src = ref.at[pl.ds(pl.multiple_of((cid - lo) * CHUNK, CHUNK), CHUNK), :]
                pltpu.make_async_copy(src, dst, sem).start()
            if len(sources) == 1:
                start()
            else:
                pl.when((cid >= lo) & (cid < lo + n))(start)
            lo += n
        return carry

    lax.fori_loop(0, n_slots, step, 0, unroll=DMA_UNROLL)


def _wait_chunk_gather(sources, dst_buf, sem):
    ref = sources[0][0]
    pltpu.make_async_copy(ref.at[pl.ds(0, dst_buf.shape[0]), :], dst_buf, sem).wait()


def _swiglu_accumulate(xb, w1_ref, w3_ref, w2_ref, acc_ref, tf):
    for c0 in range(0, tf, FF_SUB):
        c1 = min(c0 + FF_SUB, tf)
        up = jnp.dot(xb, w1_ref[:, c0:c1].astype(BF16), preferred_element_type=F32)
        lin = jnp.dot(xb, w3_ref[:, c0:c1].astype(BF16), preferred_element_type=F32)
        hid = (_silu(up) * lin).astype(BF16)
        acc_ref[...] += jnp.dot(hid, w2_ref[c0:c1, :].astype(BF16), preferred_element_type=F32)


def _moe_kernel(te_ref, src_ref, nt_ref, *refs, tf, src_chunks):
    n_src = len(src_chunks)
    hs_hbm = tuple(zip(refs[:n_src], src_chunks))
    ws_hbm = tuple(zip(refs[n_src:2 * n_src], src_chunks))
    w1_ref, w3_ref, w2_ref, ys_ref, xbuf, wbuf, acc_ref, sems = refs[2 * n_src:]
    i = pl.program_id(0)
    f = pl.program_id(1)
    n_tiles = nt_ref[0]
    valid = i < n_tiles
    slot = lax.rem(i, 2)

    def start_gather(tile, buf_slot):
        base = tile * CHUNKS_PER_TILE
        _start_chunk_gather(src_ref, base, hs_hbm, xbuf.at[buf_slot], sems.at[0, buf_slot])
        _start_chunk_gather(src_ref, base, ws_hbm, wbuf.at[buf_slot], sems.at[1, buf_slot])

    @pl.when((i == 0) & (f == 0))
    def _():
        start_gather(0, 0)

    @pl.when(valid & (f == 0))
    def _():
        _wait_chunk_gather(hs_hbm, xbuf.at[slot], sems.at[0, slot])
        _wait_chunk_gather(ws_hbm, wbuf.at[slot], sems.at[1, slot])

        @pl.when(i + 1 < n_tiles)
        def _():
            start_gather(i + 1, 1 - slot)

        acc_ref[...] = jnp.zeros_like(acc_ref)

    @pl.when(valid)
    def _():
        _swiglu_accumulate(xbuf[slot], w1_ref.at[0, 0], w3_ref.at[0, 0], w2_ref.at[0, 0], acc_ref, tf)

    @pl.when(valid & (f == pl.num_programs(1) - 1))
    def _():
        ys_ref[...] = (acc_ref[...] * wbuf[slot][:, 0:1]).astype(BF16)

    @pl.when(jnp.logical_not(valid) & (f == 0))
    def _():
        ys_ref[...] = jnp.zeros_like(ys_ref)


def _moe_grouped(hs_list, ws_list, tables, w1, w3, w2, layer, *, n_tiles_max, tf):
    tile_expert, src, n_tiles, _ = tables
    d = hs_list[0].shape[1]
    ff = w1.shape[-1]
    nf = ff // tf
    src_chunks = tuple(h.shape[0] // CHUNK for h in hs_list)
    any_spec = pl.BlockSpec(memory_space=pl.ANY)

    def w_in_map(i, f, te, sr, nt):
        return (layer, te[i], 0, jnp.where(i < nt[0], f, nf - 1))

    def w_out_map(i, f, te, sr, nt):
        return (layer, te[i], jnp.where(i < nt[0], f, nf - 1), 0)

    def y_map(i, f, te, sr, nt):
        return (i, 0)

    grid_spec = pltpu.PrefetchScalarGridSpec(
        num_scalar_prefetch=3,
        grid=(n_tiles_max, nf),
        in_specs=[any_spec] * (2 * len(hs_list)) + [
            pl.BlockSpec((1, 1, d, tf), w_in_map),
            pl.BlockSpec((1, 1, d, tf), w_in_map),
            pl.BlockSpec((1, 1, tf, d), w_out_map),
        ],
        out_specs=pl.BlockSpec((MOE_TILE, d), y_map),
        scratch_shapes=[
            pltpu.VMEM((2, MOE_TILE, d), BF16),
            pltpu.VMEM((2, MOE_TILE, LANES), F32),
            pltpu.VMEM((MOE_TILE, d), F32),
            pltpu.SemaphoreType.DMA((2, 2)),
        ],
    )
    return pl.pallas_call(
        functools.partial(_moe_kernel, tf=tf, src_chunks=src_chunks),
        grid_spec=grid_spec,
        out_shape=jax.ShapeDtypeStruct((n_tiles_max * MOE_TILE, d), BF16),
        compiler_params=pltpu.CompilerParams(
            dimension_semantics=("arbitrary", "arbitrary"),
            vmem_limit_bytes=VMEM_LIMIT),
        name="moe_grouped",
    )(tile_expert, src, n_tiles, *hs_list, *ws_list, w1, w3, w2)


def _combine_kernel(dst_ref, ys_hbm, pos_ref, x_ref, fg_ref, mod_ref, o_ref, ybuf, sems,
                    *, blocks_per_batch, mod_row, n_src_chunks, final):
    i = pl.program_id(0)
    row = i // blocks_per_batch if mod_row is None else mod_row
    buf = lax.rem(i, 2)

    source = ((ys_hbm, n_src_chunks),)

    def start_gather(blk, buf_slot):
        _start_chunk_gather(dst_ref, blk * CHUNKS_PER_BLOCK, source, ybuf.at[buf_slot], sems.at[buf_slot])

    @pl.when(i == 0)
    def _():
        start_gather(0, 0)

    @pl.when(i + 1 < pl.num_programs(0))
    def _():
        start_gather(i + 1, 1 - buf)

    stats = pos_ref[...]
    slot = lax.broadcasted_iota(jnp.int32, (stats.shape[0], BLOCK_CAP), 1).astype(F32)
    pick = jnp.where((slot == stats[:, 0:1]) | (slot == stats[:, 1:2]), 1.0, 0.0).astype(BF16)
    _wait_chunk_gather(source, ybuf.at[buf], sems.at[buf])
    moe = jnp.dot(pick, ybuf[buf], preferred_element_type=F32)
    y = x_ref[...] + _mod_row(mod_ref, row, 5) * moe
    if final:
        ms = jnp.mean(y * y, axis=-1, keepdims=True)
        y = y * lax.rsqrt(ms + EPS) * fg_ref[...]
    o_ref[...] = y


def _combine(ys, dst, pos, xall, fg, mod, *, mod_row, blocks_per_batch, final):
    t, d = xall.shape
    nblk = t // ROUTE_BLOCK
    grid_spec = pltpu.PrefetchScalarGridSpec(
        num_scalar_prefetch=1,
        grid=(nblk,),
        in_specs=[
            pl.BlockSpec(memory_space=pl.ANY),
            pl.BlockSpec((ROUTE_BLOCK, LANES), lambda i, ds: (i, 0)),
            pl.BlockSpec((ROUTE_BLOCK, d), lambda i, ds: (i, 0)),
            pl.BlockSpec((1, d), lambda i, ds: (0, 0)),
            pl.BlockSpec((SUBLANES, N_MOD * d), lambda i, ds: (0, 0)),
        ],
        out_specs=pl.BlockSpec((ROUTE_BLOCK, d), lambda i, ds: (i, 0)),
        scratch_shapes=[
            pltpu.VMEM((2, BLOCK_CAP, d), BF16),
            pltpu.SemaphoreType.DMA((2,)),
        ],
    )
    return pl.pallas_call(
        functools.partial(_combine_kernel, blocks_per_batch=blocks_per_batch, mod_row=mod_row,
                          n_src_chunks=ys.shape[0] // CHUNK, final=final),
        grid_spec=grid_spec,
        out_shape=jax.ShapeDtypeStruct((t, d), F32),
        compiler_params=pltpu.CompilerParams(
            dimension_semantics=("arbitrary",),
            vmem_limit_bytes=VMEM_LIMIT),
        name="moe_combine",
    )(dst, ys, pos, xall, fg.reshape(1, d), mod)


def _ffn_kernel(x_ref, g_ref, mod_ref, w1_ref, w3_ref, w2_ref, o_ref, h_ref, acc_ref,
                *, tiles_per_batch, mod_row, tf):
    i = pl.program_id(0)
    f = pl.program_id(1)
    row = i // tiles_per_batch if mod_row is None else mod_row

    @pl.when(f == 0)
    def _():
        h = _rms_mod(x_ref[...], g_ref[...], _mod_row(mod_ref, row, 3), _mod_row(mod_ref, row, 4))
        h_ref[...] = h.astype(BF16)
        acc_ref[...] = jnp.zeros_like(acc_ref)

    _swiglu_accumulate(h_ref[...], w1_ref.at[0], w3_ref.at[0], w2_ref.at[0], acc_ref, tf)

    @pl.when(f == pl.num_programs(1) - 1)
    def _():
        o_ref[...] = x_ref[...] + _mod_row(mod_ref, row, 5) * acc_ref[...]


def _ffn(x2, g, mod, w1, w3, w2, layer, *, tm, tf, tiles_per_batch, mod_row):
    t, d = x2.shape
    ff = w1.shape[-1]
    return pl.pallas_call(
        functools.partial(_ffn_kernel, tiles_per_batch=tiles_per_batch, mod_row=mod_row, tf=tf),
        grid=(t // tm, ff // tf),
        in_specs=[
            pl.BlockSpec((tm, d), lambda i, f: (i, 0)),
            pl.BlockSpec((1, d), lambda i, f: (0, 0)),
            pl.BlockSpec((SUBLANES, N_MOD * d), lambda i, f: (0, 0)),
            pl.BlockSpec((1, d, tf), lambda i, f: (layer, 0, f)),
            pl.BlockSpec((1, d, tf), lambda i, f: (layer, 0, f)),
            pl.BlockSpec((1, tf, d), lambda i, f: (layer, f, 0)),
        ],
        out_specs=pl.BlockSpec((tm, d), lambda i, f: (i, 0)),
        out_shape=jax.ShapeDtypeStruct((t, d), F32),
        scratch_shapes=[pltpu.VMEM((tm, d), BF16), pltpu.VMEM((tm, d), F32)],
        compiler_params=pltpu.CompilerParams(
            dimension_semantics=("arbitrary", "arbitrary"),
            vmem_limit_bytes=VMEM_LIMIT),
        name="dense_ffn",
    )(x2, g.reshape(1, d), mod, w1, w3, w2)


def _sincos_2d(rows, cols, dim):
    quarter = dim // 4
    omega = 1.0 / (10000.0 ** (jnp.arange(quarter, dtype=F32) / quarter))

    def emb1d(n):
        ang = jnp.arange(n, dtype=F32)[:, None] * omega[None, :]
        return jnp.concatenate([jnp.sin(ang), jnp.cos(ang)], axis=-1)

    er = jnp.broadcast_to(emb1d(rows)[:, None, :], (rows, cols, dim // 2))
    ec = jnp.broadcast_to(emb1d(cols)[None, :, :], (rows, cols, dim // 2))
    return jnp.concatenate([er, ec], axis=-1).reshape(rows * cols, dim)


def _block_diag_tiles(w):
    per = MXU_DIM // LRU_HEAD_DIM
    w4 = w.reshape(D_LRU // MXU_DIM, per, LRU_HEAD_DIM, LRU_HEAD_DIM)
    eye = jnp.eye(per, dtype=w.dtype)
    return jnp.einsum("ghij,hk->ghikj", w4, eye).reshape(D_LRU // MXU_DIM, MXU_DIM, MXU_DIM)


def _gate_weights(wa, wx):
    return jnp.concatenate([_block_diag_tiles(wa), _block_diag_tiles(wx)], axis=-1).astype(BF16)


def _lru_vectors(conv_w, conv_b, ba, bx, lam):
    rows = jnp.concatenate([conv_w, conv_b[None], ba[None], bx[None], lam[None]], axis=0)
    return jnp.pad(rows, ((0, 2 * SUBLANES - rows.shape[0]), (0, 0)))


def kernel(x, c, ctx, c_ctx, w_mod, b_mod, norm1_g, norm2_g, w_in, w_out, lru_conv_w, lru_conv_b, lru_wa, lru_ba, lru_wx, lru_bx, lru_lambda, conf_dw_w, conf_dw_b, conf_ln_g, conf_ln_b, ffn_w1, ffn_w3, ffn_w2, moe_router, moe_w1, moe_w3, moe_w2, final_g):
    nb, s, d = x.shape
    n_ctx = ctx.shape[1]
    pos = _sincos_2d(s // GRID_W, GRID_W, d)
    cc = jnp.concatenate([c, c_ctx[None], jnp.zeros((SUBLANES - nb - 1, d), F32)], axis=0)
    mod_all = _modulation(cc, w_mod, b_mod)

    ts_x, ts_c = 512, n_ctx
    tm_x = 1024
    tiles_per_batch = s // tm_x
    zero_state = jnp.zeros((nb, SUBLANES, D_LRU), F32)
    xc = ctx.reshape(1, nb * n_ctx, d)

    ffn_w1_b, ffn_w3_b, ffn_w2_b = (w.astype(BF16) for w in (ffn_w1, ffn_w3, ffn_w2))

    for l in range(DEPTH):
        last = l == DEPTH - 1
        mod = mod_all[l]
        w_in_b = w_in[l].astype(BF16)
        w_out_b = w_out[l].astype(BF16)
        pvec = [_lru_vectors(lru_conv_w[l], lru_conv_b[l], lru_ba[l, dr], lru_bx[l, dr],
                             lru_lambda[l, dr]) for dr in range(2)]
        wg = [_gate_weights(lru_wa[l, dr], lru_wx[l, dr]) for dr in range(2)]
        dww = jnp.repeat(conf_dw_w[l], SUBLANES, axis=0)
        cvec = jnp.pad(jnp.stack([conf_dw_b[l], conf_ln_g[l], conf_ln_b[l]]),
                       ((0, SUBLANES - 3), (0, 0)))

        u_c, hf_c, rc_c, st_f, _ = _in_forward(
            xc.reshape(nb, n_ctx, d), None, norm1_g[l], mod, w_in_b[:, :D_LRU] if last else w_in_b,
            pvec[0], wg[0], zero_state, ts=ts_c, mod_row=CTX_MOD_ROW)
        u_x, hf_x, rc_x, _, x = _in_forward(
            x, pos if l == 0 else None, norm1_g[l], mod, w_in_b, pvec[0], wg[0], st_f,
            ts=ts_x, mod_row=None)
        if last:
            _, st_b = _lru_backward(rc_c, pvec[1], wg[1], zero_state, ts=ts_c)
        else:
            xc_new, st_b = _lru_backward(
                rc_c, pvec[1], wg[1], zero_state, ts=ts_c,
                merge_args=(u_c, hf_c, xc.reshape(nb, n_ctx, d), mod, dww, cvec, w_out_b),
                mod_row=CTX_MOD_ROW)
            xc = xc_new.reshape(1, nb * n_ctx, d)
        x, _ = _lru_backward(rc_x, pvec[1], wg[1], st_b, ts=ts_x,
                             merge_args=(u_x, hf_x, x, mod, dww, cvec, w_out_b), mod_row=None)

        j = l // 2
        x2 = x.reshape(nb * s, d)
        if l % 2 == 0:
            x2 = _ffn(x2, norm2_g[l], mod, ffn_w1_b, ffn_w3_b, ffn_w2_b, j, tm=tm_x, tf=FF_BLOCK,
                      tiles_per_batch=tiles_per_batch, mod_row=None)
            xc2 = _ffn(xc[0], norm2_g[l], mod, ffn_w1_b, ffn_w3_b, ffn_w2_b, j, tm=512, tf=FF_BLOCK,
                       tiles_per_batch=1, mod_row=CTX_MOD_ROW)
            xc = xc2.reshape(1, nb * n_ctx, d)
        else:
            groups = [(x2, None)] if last else [(x2, None), (xc[0], CTX_MOD_ROW)]
            blocks_per_batch = s // ROUTE_BLOCK
            wr = jnp.pad(moe_router[j], ((0, 0), (0, LANES - N_EXPERTS)))
            routed = [_route(xg, norm2_g[l], mod, wr, mod_row=row, blocks_per_batch=blocks_per_batch)
                      for xg, row in groups]
            cnt = jnp.concatenate([r[2] for r in routed], axis=0)
            nblk = cnt.shape[0]
            n_tiles_max = (nblk * CHUNKS_PER_BLOCK + N_EXPERTS * (CHUNKS_PER_TILE - 1)) // CHUNKS_PER_TILE
            tables = _route_tables(cnt, n_tiles_max)
            ys = _moe_grouped([r[0] for r in routed], [r[1] for r in routed], tables,
                              moe_w1, moe_w3, moe_w2, j, n_tiles_max=n_tiles_max, tf=FF_BLOCK)
            outs, first = [], 0
            for (xg, row), r in zip(groups, routed):
                n_chunks = xg.shape[0] // ROUTE_BLOCK * CHUNKS_PER_BLOCK
                outs.append(_combine(ys, tables[3][first:first + n_chunks], r[3], xg, final_g, mod,
                                     mod_row=row, blocks_per_batch=blocks_per_batch, final=last))
                first += n_chunks
            x2 = outs[0]
            if not last:
                xc = outs[1].reshape(1, nb * n_ctx, d)
        x = x2.reshape(nb, s, d)
    return x
```

```python
import functools
import math

import jax
import jax.numpy as jnp
from jax import lax
from jax.experimental import pallas as pl
from jax.experimental.pallas import tpu as pltpu

F32 = jnp.float32
BF16 = jnp.bfloat16

D_MODEL = 1024
DEPTH = 4
GRID_W = 64
D_LRU = 512
D_CONV = 512
LRU_HEADS = 8
LRU_HEAD_DIM = 64
LRU_CONV_W = 4
LRU_C = 8.0
CONF_CONV_W = 31
D_FF = 2816
N_EXPERTS = 8
N_MOD = 6
EPS = 1e-6

LANES = 128
SUBLANES = 8
BF16_ROWS = 16
MXU_DIM = 256
VMEM_LIMIT = 60 * 1024 * 1024

HALO = BF16_ROWS
CONV_CHUNK = 32
CTX_MOD_ROW = 4


def _sigmoid(v):
    return 0.5 * jnp.tanh(0.5 * v) + 0.5


def _silu(v):
    h = 0.5 * v
    return h + h * jnp.tanh(h)


def _rms_mod(x, g, shift, scale):
    ms = jnp.mean(x * x, axis=-1, keepdims=True)
    y = x * lax.rsqrt(ms + EPS) * g
    return y * (1.0 + scale) + shift


def _mod_row(mod_ref, row, k):
    r = mod_ref[pl.ds(row, 1), :]
    return r[:, k * D_MODEL:(k + 1) * D_MODEL]


def _mod_kernel(cc_ref, w_ref, b_ref, o_ref):
    s = _silu(cc_ref[...])
    o_ref[0] = jnp.dot(s.astype(BF16), w_ref[0].astype(BF16),
                       preferred_element_type=F32) + b_ref[0]


def _modulation(cc, w_mod, b_mod):
    n_out = N_MOD * D_MODEL
    tn = 1536
    return pl.pallas_call(
        _mod_kernel,
        grid=(DEPTH, n_out // tn),
        in_specs=[
            pl.BlockSpec((SUBLANES, D_MODEL), lambda l, j: (0, 0)),
            pl.BlockSpec((1, D_MODEL, tn), lambda l, j: (l, 0, j)),
            pl.BlockSpec((1, 1, tn), lambda l, j: (l, 0, j)),
        ],
        out_specs=pl.BlockSpec((1, SUBLANES, tn), lambda l, j: (l, 0, j)),
        out_shape=jax.ShapeDtypeStruct((DEPTH, SUBLANES, n_out), F32),
        compiler_params=pltpu.CompilerParams(
            dimension_semantics=("arbitrary", "arbitrary"),
            vmem_limit_bytes=VMEM_LIMIT),
        name="modulation",
    )(cc, w_mod, b_mod.reshape(DEPTH, 1, n_out))


def _short_conv(prev, main, nxt, pvec_ref, ts):
    groups = ts // SUBLANES
    ext = jnp.concatenate([prev, main, nxt], axis=0)
    x3 = ext.reshape(groups + 2, SUBLANES, D_LRU)
    row = lax.broadcasted_iota(jnp.int32, (groups, SUBLANES, D_LRU), 1)

    def window(off):
        if off == 0:
            return x3[1:groups + 1]
        r = pltpu.roll(x3, (-off) % SUBLANES, axis=1)
        if off < 0:
            return jnp.where(row >= -off, r[1:groups + 1], r[0:groups])
        return jnp.where(row < SUBLANES - off, r[1:groups + 1], r[2:groups + 2])

    rc = jnp.zeros((groups, SUBLANES, D_LRU), F32) + pvec_ref[4:5, :]
    for k in range(LRU_CONV_W):
        rc = rc + pvec_ref[k:k + 1, :] * window(k - 2)
    return rc.reshape(ts, D_LRU)


def _lru_gates(rc, pvec_ref, wg_ref):
    rcb = rc.astype(BF16)
    ga, gx = [], []
    for j in range(D_LRU // MXU_DIM):
        o = jnp.dot(rcb[:, j * MXU_DIM:(j + 1) * MXU_DIM], wg_ref[j],
                    preferred_element_type=F32)
        ga.append(o[:, :MXU_DIM])
        gx.append(o[:, MXU_DIM:])
    r_gate = _sigmoid(jnp.concatenate(ga, axis=-1) + pvec_ref[5:6, :])
    i_gate = _sigmoid(jnp.concatenate(gx, axis=-1) + pvec_ref[6:7, :])
    neg_lam = -pvec_ref[7:8, :]
    softplus = jnp.maximum(neg_lam, 0.0) + jnp.log1p(jnp.exp(-jnp.abs(neg_lam)))
    log_a = -LRU_C * r_gate * softplus
    a = jnp.exp(log_a)
    one_minus_a2 = -jnp.tanh(log_a) * (a * a + 1.0)
    b = jnp.sqrt(one_minus_a2) * (i_gate * rc)
    return a, b


def _lru_scan(a, b, h_in, hs_ref, ts, reverse):
    groups = ts // SUBLANES
    c = a.shape[-1]
    a3 = a.reshape(groups, SUBLANES, c)
    b3 = b.reshape(groups, SUBLANES, c)
    row = lax.broadcasted_iota(jnp.int32, (groups, SUBLANES, c), 1)
    for s in (1, 2, 4):
        shift = SUBLANES - s if reverse else s
        a_sh = pltpu.roll(a3, shift, axis=1)
        b_sh = pltpu.roll(b3, shift, axis=1)
        m = (row < SUBLANES - s) if reverse else (row >= s)
        b3 = jnp.where(m, a3 * b_sh + b3, b3)
        a3 = jnp.where(m, a3 * a_sh, a3)
    h = h_in
    order = range(groups - 1, -1, -1) if reverse else range(groups)
    edge = 0 if reverse else SUBLANES - 1
    for g in order:
        hg = a3[g] * h + b3[g]
        hs_ref[g * SUBLANES:(g + 1) * SUBLANES, :] = hg
        h = jnp.broadcast_to(hg[edge:edge + 1, :], (SUBLANES, c))
    return h


def _halo_specs(width, col, ts, s, tile_of):
    per = ts // HALO
    last = s // HALO - 1
    main = pl.BlockSpec((1, ts, width), lambda b, t: (b, tile_of(t), col))
    prev = pl.BlockSpec((1, HALO, width),
                        lambda b, t: (b, jnp.maximum(tile_of(t) * per - 1, 0), col))
    nxt = pl.BlockSpec((1, HALO, width),
                       lambda b, t: (b, jnp.minimum((tile_of(t) + 1) * per, last), col))
    return [main, prev, nxt]


def _in_fwd_kernel(*refs, ts, add_pos, mod_row):
    if add_pos:
        (x_ref, xp_ref, xn_ref, pos_ref, posp_ref, posn_ref, g_ref, mod_ref, w_ref, pvec_ref, wg_ref,
         seed_ref, u_ref, hf_ref, rc_ref, state_ref, xo_ref, hs_ref, carry_ref) = refs
    else:
        (x_ref, xp_ref, xn_ref, g_ref, mod_ref, w_ref, pvec_ref, wg_ref,
         seed_ref, u_ref, hf_ref, rc_ref, state_ref, hs_ref, carry_ref) = refs
    t = pl.program_id(1)
    nt = pl.num_programs(1)
    row = pl.program_id(0) if mod_row is None else mod_row

    @pl.when(t == 0)
    def _():
        carry_ref[...] = seed_ref[0]

    g = g_ref[...]
    shift = _mod_row(mod_ref, row, 0)
    scale = _mod_row(mod_ref, row, 1)
    x, xp, xn = x_ref[0], xp_ref[0], xn_ref[0]
    if add_pos:
        x, xp, xn = x + pos_ref[...], xp + posp_ref[...], xn + posn_ref[...]
        xo_ref[0] = x
    u = jnp.dot(_rms_mod(x, g, shift, scale).astype(BF16), w_ref[...], preferred_element_type=F32)
    u_ref[0] = u.astype(BF16)

    def halo_rows(xh, keep):
        h = _rms_mod(xh, g, shift, scale).astype(BF16)
        return jnp.dot(h, w_ref[:, 0:D_LRU], preferred_element_type=F32) * keep

    prev = halo_rows(xp, (t > 0).astype(F32))[HALO - SUBLANES:HALO, :]
    nxt = halo_rows(xn, (t < nt - 1).astype(F32))[0:SUBLANES, :]
    rc = _short_conv(prev, u[:, 0:D_LRU], nxt, pvec_ref, ts)
    rc_ref[0] = rc
    a, b = _lru_gates(rc, pvec_ref, wg_ref)
    h = _lru_scan(a, b, carry_ref[...], hs_ref, ts, reverse=False)
    carry_ref[...] = h
    state_ref[0] = h
    hf_ref[0] = hs_ref[...].astype(BF16)


def _in_forward(x, pos, g, mod, w, pvec, wg, seed, *, ts, mod_row):
    nb, s, d = x.shape
    n = w.shape[1]
    add_pos = pos is not None
    per = ts // HALO
    last = s // HALO - 1

    def prev_blk(t):
        return jnp.maximum(t * per - 1, 0)

    def next_blk(t):
        return jnp.minimum((t + 1) * per, last)

    in_specs = [
        pl.BlockSpec((1, ts, d), lambda b, t: (b, t, 0)),
        pl.BlockSpec((1, HALO, d), lambda b, t: (b, prev_blk(t), 0)),
        pl.BlockSpec((1, HALO, d), lambda b, t: (b, next_blk(t), 0)),
    ]
    args = [x, x, x]
    if add_pos:
        in_specs += [
            pl.BlockSpec((ts, d), lambda b, t: (t, 0)),
            pl.BlockSpec((HALO, d), lambda b, t: (prev_blk(t), 0)),
            pl.BlockSpec((HALO, d), lambda b, t: (next_blk(t), 0)),
        ]
        args += [pos, pos, pos]
    in_specs += [
        pl.BlockSpec((1, d), lambda b, t: (0, 0)),
        pl.BlockSpec((SUBLANES, N_MOD * d), lambda b, t: (0, 0)),
        pl.BlockSpec((d, n), lambda b, t: (0, 0)),
        pl.BlockSpec((2 * SUBLANES, D_LRU), lambda b, t: (0, 0)),
        pl.BlockSpec((D_LRU // MXU_DIM, MXU_DIM, 2 * MXU_DIM), lambda b, t: (0, 0, 0)),
        pl.BlockSpec((1, SUBLANES, D_LRU), lambda b, t: (b, 0, 0)),
    ]
    args += [g.reshape(1, d), mod, w, pvec, wg, seed]
    out_specs = [
        pl.BlockSpec((1, ts, n), lambda b, t: (b, t, 0)),
        pl.BlockSpec((1, ts, D_LRU), lambda b, t: (b, t, 0)),
        pl.BlockSpec((1, ts, D_LRU), lambda b, t: (b, t, 0)),
        pl.BlockSpec((1, SUBLANES, D_LRU), lambda b, t: (b, 0, 0)),
    ]
    out_shape = [
        jax.ShapeDtypeStruct((nb, s, n), BF16),
        jax.ShapeDtypeStruct((nb, s, D_LRU), BF16),
        jax.ShapeDtypeStruct((nb, s, D_LRU), F32),
        jax.ShapeDtypeStruct((nb, SUBLANES, D_LRU), F32),
    ]
    if add_pos:
        out_specs.append(pl.BlockSpec((1, ts, d), lambda b, t: (b, t, 0)))
        out_shape.append(jax.ShapeDtypeStruct((nb, s, d), F32))
    res = pl.pallas_call(
        functools.partial(_in_fwd_kernel, ts=ts, add_pos=add_pos, mod_row=mod_row),
        grid=(nb, s // ts),
        in_specs=in_specs,
        out_specs=out_specs,
        out_shape=out_shape,
        scratch_shapes=[
            pltpu.VMEM((ts, D_LRU), F32),
            pltpu.VMEM((SUBLANES, D_LRU), F32),
        ],
        compiler_params=pltpu.CompilerParams(
            dimension_semantics=("arbitrary", "arbitrary"),
            vmem_limit_bytes=VMEM_LIMIT),
        name="in_forward",
    )(*args)
    u, hf, rc, state = res[:4]
    return u, hf, rc, state, (res[4] if add_pos else x)


def _bwd_kernel(*refs, ts, merge, mod_row):
    if merge:
        (rc_ref, pvec_ref, wg_ref, seed_ref,
         gate_ref, v_ref, vprev_ref, vnext_ref, hf_ref, x_ref, mod_ref,
         dww_ref, cvec_ref, wo_ref,
         xo_ref, state_ref,
         hs_ref, carry_ref, g_ref, cs_ref, sh_ref) = refs
    else:
        (rc_ref, pvec_ref, wg_ref, seed_ref,
         state_ref, hs_ref, carry_ref) = refs
    tg = pl.program_id(1)
    nt = pl.num_programs(1)
    t = nt - 1 - tg

    @pl.when(tg == 0)
    def _():
        carry_ref[...] = seed_ref[0]

    has_prev = (t > 0).astype(F32)
    has_next = (t < nt - 1).astype(F32)
    a, b = _lru_gates(rc_ref[0], pvec_ref, wg_ref)
    h = _lru_scan(a, b, carry_ref[...], hs_ref, ts, reverse=True)
    carry_ref[...] = h
    state_ref[0] = h
    if not merge:
        return

    y = hf_ref[0].astype(F32) + hs_ref[...]
    gate = gate_ref[0].astype(F32)
    cdf = 0.5 * (1.0 + jnp.tanh(math.sqrt(2.0 / math.pi) * (gate + 0.044715 * (gate * gate * gate))))
    lru = (y * (gate * cdf)).astype(BF16)

    def glu(vref):
        v = vref[0].astype(F32)
        return v[:, :D_CONV] * _sigmoid(v[:, D_CONV:])

    g_ref[0:HALO, :] = glu(vprev_ref) * has_prev
    g_ref[HALO:HALO + ts, :] = glu(v_ref)
    g_ref[HALO + ts:2 * HALO + ts, :] = glu(vnext_ref) * has_next
    pad = (CONF_CONV_W - 1) // 2
    dwb = cvec_ref[0:1, :]
    span = ts + 2 * HALO - SUBLANES
    for r in range(1, SUBLANES):
        sh_ref[r - 1, 0:span, :] = g_ref[pl.ds(r, span), :]

    def conv_chunk(i, carry):
        base = pl.multiple_of(i * CONV_CHUNK, CONV_CHUNK)
        groups = CONV_CHUNK // SUBLANES
        acc = jnp.zeros((groups, SUBLANES, D_CONV), F32) + dwb
        for k in range(CONF_CONV_W):
            q, r = divmod(HALO - pad + k, SUBLANES)
            rows = pl.ds(base + q * SUBLANES, CONV_CHUNK)
            tap = g_ref[rows, :] if r == 0 else sh_ref[r - 1, rows, :]
            wk = dww_ref[k * SUBLANES:(k + 1) * SUBLANES, :]
            acc = acc + wk[None] * tap.reshape(groups, SUBLANES, D_CONV)
        cs_ref[pl.ds(base, CONV_CHUNK), :] = acc.reshape(CONV_CHUNK, D_CONV)
        return carry

    lax.fori_loop(0, ts // CONV_CHUNK, conv_chunk, 0)
    cv = cs_ref[...]
    mu = jnp.mean(cv, axis=-1, keepdims=True)
    var = jnp.mean(jnp.square(cv - mu), axis=-1, keepdims=True)
    ln = (cv - mu) * lax.rsqrt(var + EPS) * cvec_ref[1:2, :] + cvec_ref[2:3, :]
    conf = _silu(ln).astype(BF16)

    o = jnp.dot(lru, wo_ref[0:D_LRU, :], preferred_element_type=F32)
    o = o + jnp.dot(conf, wo_ref[D_LRU:, :], preferred_element_type=F32)
    row = pl.program_id(0) if mod_row is None else mod_row
    xo_ref[0] = x_ref[0] + _mod_row(mod_ref, row, 2) * o


def _lru_backward(rc, pvec, wg, seed, *, ts, merge_args=None, mod_row=None):
    nb, s, _ = rc.shape
    nt = s // ts
    rev = lambda t: nt - 1 - t
    in_specs = [
        pl.BlockSpec((1, ts, D_LRU), lambda b, t: (b, rev(t), 0)),
        pl.BlockSpec((2 * SUBLANES, D_LRU), lambda b, t: (0, 0)),
        pl.BlockSpec((D_LRU // MXU_DIM, MXU_DIM, 2 * MXU_DIM), lambda b, t: (0, 0, 0)),
        pl.BlockSpec((1, SUBLANES, D_LRU), lambda b, t: (b, 0, 0)),
    ]
    args = [rc, pvec, wg, seed]
    state_spec = pl.BlockSpec((1, SUBLANES, D_LRU), lambda b, t: (b, 0, 0))
    state_shape = jax.ShapeDtypeStruct((nb, SUBLANES, D_LRU), F32)
    scratch = [
        pltpu.VMEM((ts, D_LRU), F32),
        pltpu.VMEM((SUBLANES, D_LRU), F32),
    ]
    merge = merge_args is not None
    if merge:
        u, hf, x, mod, dww, cvec, wo = merge_args
        d = x.shape[-1]
        in_specs += [pl.BlockSpec((1, ts, D_LRU), lambda b, t: (b, rev(t), 1))]
        in_specs += _halo_specs(2 * D_CONV, 1, ts, s, rev)
        in_specs += [
            pl.BlockSpec((1, ts, D_LRU), lambda b, t: (b, rev(t), 0)),
            pl.BlockSpec((1, ts, d), lambda b, t: (b, rev(t), 0)),
            pl.BlockSpec((SUBLANES, N_MOD * d), lambda b, t: (0, 0)),
            pl.BlockSpec((CONF_CONV_W * SUBLANES, D_CONV), lambda b, t: (0, 0)),
            pl.BlockSpec((SUBLANES, D_CONV), lambda b, t: (0, 0)),
            pl.BlockSpec((d, d), lambda b, t: (0, 0)),
        ]
        args += [u, u, u, u, hf, x, mod, dww, cvec, wo]
        out_specs = [pl.BlockSpec((1, ts, d), lambda b, t: (b, rev(t), 0)), state_spec]
        out_shape = [jax.ShapeDtypeStruct((nb, s, d), F32), state_shape]
        scratch += [
            pltpu.VMEM((ts + 2 * HALO, D_CONV), F32),
            pltpu.VMEM((ts, D_CONV), F32),
            pltpu.VMEM((SUBLANES - 1, ts + 2 * HALO, D_CONV), F32),
        ]
    else:
        out_specs = [state_spec]
        out_shape = [state_shape]
    res = pl.pallas_call(
        functools.partial(_bwd_kernel, ts=ts, merge=merge, mod_row=mod_row),
        grid=(nb, nt),
        in_specs=in_specs,
        out_specs=out_specs,
        out_shape=out_shape,
        scratch_shapes=scratch,
        compiler_params=pltpu.CompilerParams(
            dimension_semantics=("arbitrary", "arbitrary"),
            vmem_limit_bytes=VMEM_LIMIT),
        name="lru_backward_merge" if merge else "lru_backward",
    )(*args)
    return res if merge else (None, res[0])


ROUTE_BLOCK = 512
CHUNK = BF16_ROWS
BLOCK_CAP = 2 * ROUTE_BLOCK + LANES
CHUNKS_PER_BLOCK = BLOCK_CAP // CHUNK
MOE_TILE = 1024
FF_BLOCK = 6 * MXU_DIM
FF_SUB = MXU_DIM
CHUNKS_PER_TILE = MOE_TILE // CHUNK


def _route_kernel(x_ref, g_ref, mod_ref, wr_ref, hs_ref, ws_ref, cnt_ref, pos_ref,
                  *, blocks_per_batch, mod_row):
    i = pl.program_id(0)
    row = i // blocks_per_batch if mod_row is None else mod_row
    h = _rms_mod(x_ref[...], g_ref[...], _mod_row(mod_ref, row, 3), _mod_row(mod_ref, row, 4))
    wr = wr_ref[...]
    h_hi = h.astype(BF16)
    h_lo = (h - h_hi.astype(F32)).astype(BF16)
    w_hi = wr.astype(BF16)
    w_lo = (wr - w_hi.astype(F32)).astype(BF16)
    logits = (jnp.dot(h_hi, w_hi, preferred_element_type=F32)
              + jnp.dot(h_hi, w_lo, preferred_element_type=F32)
              + jnp.dot(h_lo, w_hi, preferred_element_type=F32))
    lane = lax.broadcasted_iota(jnp.int32, logits.shape, 1).astype(F32)
    neg = jnp.float32(-jnp.inf)
    l1 = jnp.where(lane < N_EXPERTS, logits, neg)
    m1 = jnp.max(l1, axis=-1, keepdims=True)
    i1 = jnp.min(jnp.where(l1 == m1, lane, float(LANES)), axis=-1, keepdims=True)
    l2 = jnp.where(lane == i1, neg, l1)
    m2 = jnp.max(l2, axis=-1, keepdims=True)
    i2 = jnp.min(jnp.where(l2 == m2, lane, float(LANES)), axis=-1, keepdims=True)
    e = jnp.exp(m2 - m1)
    p1 = 1.0 / (1.0 + e)
    p2 = e / (1.0 + e)

    sel1 = lane == i1
    sel2 = lane == i2
    member = jnp.where(sel1 | sel2, 1.0, 0.0)
    cnt = jnp.sum(member, axis=0, keepdims=True)
    padded = jnp.floor((cnt + (CHUNK - 1.0)) * (1.0 / CHUNK)) * CHUNK
    r128 = lax.broadcasted_iota(jnp.int32, (LANES, LANES), 0)
    c128 = lax.broadcasted_iota(jnp.int32, (LANES, LANES), 1)
    before = jnp.where(r128 < c128, 1.0, 0.0)
    seg_off = jnp.dot(jnp.broadcast_to(padded, (SUBLANES, LANES)), before,
                      preferred_element_type=F32, precision=lax.Precision.HIGHEST)[0:1, :]
    bt = member.shape[0]
    rt = lax.broadcasted_iota(jnp.int32, (bt, bt), 0)
    ct = lax.broadcasted_iota(jnp.int32, (bt, bt), 1)
    earlier = jnp.where(ct < rt, 1.0, 0.0).astype(BF16)
    rank = jnp.dot(earlier, member.astype(BF16), preferred_element_type=F32)
    slot_of = seg_off + rank
    pos1 = jnp.sum(jnp.where(sel1, slot_of, 0.0), axis=-1, keepdims=True)
    pos2 = jnp.sum(jnp.where(sel2, slot_of, 0.0), axis=-1, keepdims=True)
    stats = (jnp.where(lane == 0.0, pos1, 0.0) + jnp.where(lane == 1.0, pos2, 0.0)
             + jnp.where(lane == 2.0, p1, 0.0) + jnp.where(lane == 3.0, p2, 0.0))
    pos_ref[...] = stats
    cnt_ref[0] = jnp.broadcast_to(cnt, (SUBLANES, LANES))

    st = stats.T
    slot = lax.broadcasted_iota(jnp.int32, (BLOCK_CAP, bt), 0).astype(F32)
    hit1 = slot == st[0:1, :]
    hit2 = slot == st[1:2, :]
    perm = jnp.where(hit1 | hit2, 1.0, 0.0).astype(BF16)
    hs_ref[...] = jnp.dot(perm, h.astype(BF16), preferred_element_type=F32).astype(BF16)
    wrow = jnp.where(hit1, st[2:3, :], 0.0) + jnp.where(hit2, st[3:4, :], 0.0)
    ws_ref[...] = jnp.broadcast_to(jnp.sum(wrow, axis=-1, keepdims=True), (BLOCK_CAP, LANES))


def _route(xall, g, mod, wr, *, mod_row, blocks_per_batch):
    t, d = xall.shape
    nblk = t // ROUTE_BLOCK
    return pl.pallas_call(
        functools.partial(_route_kernel, blocks_per_batch=blocks_per_batch,
                          mod_row=mod_row),
        grid=(nblk,),
        in_specs=[
            pl.BlockSpec((ROUTE_BLOCK, d), lambda i: (i, 0)),
            pl.BlockSpec((1, d), lambda i: (0, 0)),
            pl.BlockSpec((SUBLANES, N_MOD * d), lambda i: (0, 0)),
            pl.BlockSpec((d, LANES), lambda i: (0, 0)),
        ],
        out_specs=[
            pl.BlockSpec((BLOCK_CAP, d), lambda i: (i, 0)),
            pl.BlockSpec((BLOCK_CAP, LANES), lambda i: (i, 0)),
            pl.BlockSpec((1, SUBLANES, LANES), lambda i: (i, 0, 0)),
            pl.BlockSpec((ROUTE_BLOCK, LANES), lambda i: (i, 0)),
        ],
        out_shape=[
            jax.ShapeDtypeStruct((nblk * BLOCK_CAP, d), BF16),
            jax.ShapeDtypeStruct((nblk * BLOCK_CAP, LANES), F32),
            jax.ShapeDtypeStruct((nblk, SUBLANES, LANES), F32),
            jax.ShapeDtypeStruct((t, LANES), F32),
        ],
        compiler_params=pltpu.CompilerParams(
            dimension_semantics=("arbitrary",),
            vmem_limit_bytes=VMEM_LIMIT),
        name="route",
    )(xall, g.reshape(1, d), mod, wr)


def _route_tables(cnt, n_tiles_max):
    nblk = cnt.shape[0]
    n_be = (cnt[:, 0, :N_EXPERTS].astype(jnp.int32) + (CHUNK - 1)) // CHUNK
    ends_be = jnp.cumsum(n_be, axis=1)
    off_be = ends_be - n_be
    start_be = jnp.cumsum(n_be, axis=0) - n_be
    g_e = jnp.sum(n_be, axis=0)
    tiles_e = (g_e + (CHUNKS_PER_TILE - 1)) // CHUNKS_PER_TILE
    tend_e = jnp.cumsum(tiles_e)
    tstart_e = tend_e - tiles_e
    n_tiles = tend_e[-1]

    experts = jnp.arange(N_EXPERTS, dtype=jnp.int32)

    def pick(onehot, table):
        return jnp.sum(jnp.where(onehot, table, 0), axis=-1)

    c = jnp.arange(CHUNKS_PER_BLOCK, dtype=jnp.int32)
    e_bc = jnp.sum((c[None, :, None] >= ends_be[:, None, :]).astype(jnp.int32), axis=-1)
    oh_bc = e_bc[:, :, None] == experts
    shift_be = tstart_e[None, :] * CHUNKS_PER_TILE + start_be - off_be
    d_bc = pick(oh_bc, shift_be[:, None, :]) + c[None, :]
    dst = jnp.where(e_bc < N_EXPERTS, d_bc, -1).reshape(-1).astype(jnp.int32)

    ti = jnp.arange(n_tiles_max, dtype=jnp.int32)
    ti_cl = jnp.minimum(ti, n_tiles - 1)
    tile_expert = jnp.minimum(
        jnp.sum((ti_cl[:, None] >= tend_e[None, :]).astype(jnp.int32), axis=-1), N_EXPERTS - 1)
    oh_t = tile_expert[:, None] == experts
    first_q = (ti - pick(oh_t, tstart_e[None, :])) * CHUNKS_PER_TILE

    q = first_q[:, None] + jnp.arange(CHUNKS_PER_TILE, dtype=jnp.int32)[None, :]
    cum_tb = pick(oh_t[:, None, :], (start_be + n_be)[None, :, :])
    b_tj = jnp.minimum(jnp.sum((q[:, :, None] >= cum_tb[:, None, :]).astype(jnp.int32), axis=-1),
                       nblk - 1)
    oh_b = b_tj[:, :, None] == jnp.arange(nblk, dtype=jnp.int32)
    local_tb = pick(oh_t[:, None, :], (off_be - start_be)[None, :, :])
    src = b_tj * CHUNKS_PER_BLOCK + q + pick(oh_b, local_tb[:, None, :])
    src = jnp.clip(src, 0, nblk * CHUNKS_PER_BLOCK - 1).reshape(-1).astype(jnp.int32)
    return tile_expert.astype(jnp.int32), src, n_tiles.reshape(1).astype(jnp.int32), dst


DMA_UNROLL = 8


def _start_chunk_gather(table_ref, base, sources, dst_buf, sem):
    n_slots = dst_buf.shape[0] // CHUNK

    def step(j, carry):
        cid = jnp.maximum(table_ref[base + j], 0)
        dst = dst_buf.at[pl.ds(pl.multiple_of(j * CHUNK, CHUNK), CHUNK), :]
        lo = 0
        for ref, n in sources:
            def start(ref=ref, lo=lo):
                src = ref.at[pl.ds(pl.multiple_of((cid - lo) * CHUNK, CHUNK), CHUNK), :]
                pltpu.make_async_copy(src, dst, sem).start()
            if len(sources) == 1:
                start()
            else:
                pl.when((cid >= lo) & (cid < lo + n))(start)
            lo += n
        return carry

    lax.fori_loop(0, n_slots, step, 0, unroll=DMA_UNROLL)


def _wait_chunk_gather(sources, dst_buf, sem):
    ref = sources[0][0]
    pltpu.make_async_copy(ref.at[pl.ds(0, dst_buf.shape[0]), :], dst_buf, sem).wait()


def _swiglu_accumulate(load_x, w1_ref, w3_ref, w2_ref, acc_ref, f, ff):
    last_valid = (ff - (pl.cdiv(ff, FF_BLOCK) - 1) * FF_BLOCK) // FF_SUB
    is_last = f == pl.cdiv(ff, FF_BLOCK) - 1
    for c in range(FF_BLOCK // FF_SUB):
        cols = slice(c * FF_SUB, (c + 1) * FF_SUB)

        def chunk(cols=cols):
            xb = load_x()
            up = jnp.dot(xb, w1_ref[:, cols].astype(BF16), preferred_element_type=F32)
            lin = jnp.dot(xb, w3_ref[:, cols].astype(BF16), preferred_element_type=F32)
            hid = (_silu(up) * lin).astype(BF16)
            acc_ref[...] += jnp.dot(hid, w2_ref[cols, :].astype(BF16), preferred_element_type=F32)

        if c < last_valid:
            chunk()
        else:
            pl.when(jnp.logical_not(is_last))(chunk)


def _moe_kernel(te_ref, src_ref, nt_ref, *refs, ff, src_chunks):
    n_src = len(src_chunks)
    hs_hbm = tuple(zip(refs[:n_src], src_chunks))
    ws_hbm = tuple(zip(refs[n_src:2 * n_src], src_chunks))
    w1_ref, w3_ref, w2_ref, ys_ref, xbuf, wbuf, acc_ref, sems = refs[2 * n_src:]
    i = pl.program_id(0)
    f = pl.program_id(1)
    n_tiles = nt_ref[0]
    valid = i < n_tiles
    slot = lax.rem(i, 2)

    def start_gather(tile, buf_slot):
        base = tile * CHUNKS_PER_TILE
        _start_chunk_gather(src_ref, base, hs_hbm, xbuf.at[buf_slot], sems.at[0, buf_slot])
        _start_chunk_gather(src_ref, base, ws_hbm, wbuf.at[buf_slot], sems.at[1, buf_slot])

    @pl.when((i == 0) & (f == 0))
    def _():
        start_gather(0, 0)

    @pl.when(valid & (f == 0))
    def _():
        _wait_chunk_gather(hs_hbm, xbuf.at[slot], sems.at[0, slot])
        _wait_chunk_gather(ws_hbm, wbuf.at[slot], sems.at[1, slot])

        @pl.when(i + 1 < n_tiles)
        def _():
            start_gather(i + 1, 1 - slot)

        acc_ref[...] = jnp.zeros_like(acc_ref)

    @pl.when(valid)
    def _():
        _swiglu_accumulate(lambda: xbuf[slot], w1_ref.at[0, 0], w3_ref.at[0, 0], w2_ref.at[0, 0],
                           acc_ref, f, ff)

    @pl.when(valid & (f == pl.num_programs(1) - 1))
    def _():
        ys_ref[...] = (acc_ref[...] * wbuf[slot][:, 0:1]).astype(BF16)

    @pl.when(jnp.logical_not(valid) & (f == 0))
    def _():
        ys_ref[...] = jnp.zeros_like(ys_ref)


def _moe_grouped(hs_list, ws_list, tables, w1, w3, w2, layer, *, n_tiles_max):
    tile_expert, src, n_tiles, _ = tables
    d = hs_list[0].shape[1]
    ff = w1.shape[-1]
    tf = FF_BLOCK
    nf = pl.cdiv(ff, tf)
    src_chunks = tuple(h.shape[0] // CHUNK for h in hs_list)
    any_spec = pl.BlockSpec(memory_space=pl.ANY)

    def w_in_map(i, f, te, sr, nt):
        return (layer, te[i], 0, jnp.where(i < nt[0], f, nf - 1))

    def w_out_map(i, f, te, sr, nt):
        return (layer, te[i], jnp.where(i < nt[0], f, nf - 1), 0)

    def y_map(i, f, te, sr, nt):
        return (i, 0)

    grid_spec = pltpu.PrefetchScalarGridSpec(
        num_scalar_prefetch=3,
        grid=(n_tiles_max, nf),
        in_specs=[any_spec] * (2 * len(hs_list)) + [
            pl.BlockSpec((1, 1, d, tf), w_in_map),
            pl.BlockSpec((1, 1, d, tf), w_in_map),
            pl.BlockSpec((1, 1, tf, d), w_out_map),
        ],
        out_specs=pl.BlockSpec((MOE_TILE, d), y_map),
        scratch_shapes=[
            pltpu.VMEM((2, MOE_TILE, d), BF16),
            pltpu.VMEM((2, MOE_TILE, LANES), F32),
            pltpu.VMEM((MOE_TILE, d), F32),
            pltpu.SemaphoreType.DMA((2, 2)),
        ],
    )
    return pl.pallas_call(
        functools.partial(_moe_kernel, ff=ff, src_chunks=src_chunks),
        grid_spec=grid_spec,
        out_shape=jax.ShapeDtypeStruct((n_tiles_max * MOE_TILE, d), BF16),
        compiler_params=pltpu.CompilerParams(
            dimension_semantics=("arbitrary", "arbitrary"),
            vmem_limit_bytes=VMEM_LIMIT),
        name="moe_grouped",
    )(tile_expert, src, n_tiles, *hs_list, *ws_list, w1, w3, w2)


def _combine_kernel(dst_ref, ys_hbm, pos_ref, x_ref, fg_ref, mod_ref, o_ref, ybuf, sems,
                    *, blocks_per_batch, mod_row, n_src_chunks, final):
    i = pl.program_id(0)
    row = i // blocks_per_batch if mod_row is None else mod_row
    buf = lax.rem(i, 2)

    source = ((ys_hbm, n_src_chunks),)

    def start_gather(blk, buf_slot):
        _start_chunk_gather(dst_ref, blk * CHUNKS_PER_BLOCK, source, ybuf.at[buf_slot], sems.at[buf_slot])

    @pl.when(i == 0)
    def _():
        start_gather(0, 0)

    @pl.when(i + 1 < pl.num_programs(0))
    def _():
        start_gather(i + 1, 1 - buf)

    stats = pos_ref[...]
    slot = lax.broadcasted_iota(jnp.int32, (stats.shape[0], BLOCK_CAP), 1).astype(F32)
    pick = jnp.where((slot == stats[:, 0:1]) | (slot == stats[:, 1:2]), 1.0, 0.0).astype(BF16)
    _wait_chunk_gather(source, ybuf.at[buf], sems.at[buf])
    moe = jnp.dot(pick, ybuf[buf], preferred_element_type=F32)
    y = x_ref[...] + _mod_row(mod_ref, row, 5) * moe
    if final:
        ms = jnp.mean(y * y, axis=-1, keepdims=True)
        y = y * lax.rsqrt(ms + EPS) * fg_ref[...]
    o_ref[...] = y


def _combine(ys, dst, pos, xall, fg, mod, *, mod_row, blocks_per_batch, final):
    t, d = xall.shape
    nblk = t // ROUTE_BLOCK
    grid_spec = pltpu.PrefetchScalarGridSpec(
        num_scalar_prefetch=1,
        grid=(nblk,),
        in_specs=[
            pl.BlockSpec(memory_space=pl.ANY),
            pl.BlockSpec((ROUTE_BLOCK, LANES), lambda i, ds: (i, 0)),
            pl.BlockSpec((ROUTE_BLOCK, d), lambda i, ds: (i, 0)),
            pl.BlockSpec((1, d), lambda i, ds: (0, 0)),
            pl.BlockSpec((SUBLANES, N_MOD * d), lambda i, ds: (0, 0)),
        ],
        out_specs=pl.BlockSpec((ROUTE_BLOCK, d), lambda i, ds: (i, 0)),
        scratch_shapes=[
            pltpu.VMEM((2, BLOCK_CAP, d), BF16),
            pltpu.SemaphoreType.DMA((2,)),
        ],
    )
    return pl.pallas_call(
        functools.partial(_combine_kernel, blocks_per_batch=blocks_per_batch, mod_row=mod_row,
                          n_src_chunks=ys.shape[0] // CHUNK, final=final),
        grid_spec=grid_spec,
        out_shape=jax.ShapeDtypeStruct((t, d), F32),
        compiler_params=pltpu.CompilerParams(
            dimension_semantics=("arbitrary",),
            vmem_limit_bytes=VMEM_LIMIT),
        name="moe_combine",
    )(dst, ys, pos, xall, fg.reshape(1, d), mod)


def _ffn_kernel(x_ref, g_ref, mod_ref, w1_ref, w3_ref, w2_ref, o_ref, h_ref, acc_ref,
                *, tiles_per_batch, mod_row, ff):
    i = pl.program_id(0)
    f = pl.program_id(1)
    row = i // tiles_per_batch if mod_row is None else mod_row

    @pl.when(f == 0)
    def _():
        h = _rms_mod(x_ref[...], g_ref[...], _mod_row(mod_ref, row, 3), _mod_row(mod_ref, row, 4))
        h_ref[...] = h.astype(BF16)
        acc_ref[...] = jnp.zeros_like(acc_ref)

    _swiglu_accumulate(lambda: h_ref[...], w1_ref.at[0], w3_ref.at[0], w2_ref.at[0], acc_ref, f, ff)

    @pl.when(f == pl.num_programs(1) - 1)
    def _():
        o_ref[...] = x_ref[...] + _mod_row(mod_ref, row, 5) * acc_ref[...]


def _ffn(x2, g, mod, w1, w3, w2, layer, *, tm, tiles_per_batch, mod_row):
    t, d = x2.shape
    ff = w1.shape[-1]
    tf = FF_BLOCK
    return pl.pallas_call(
        functools.partial(_ffn_kernel, tiles_per_batch=tiles_per_batch, mod_row=mod_row, ff=ff),
        grid=(t // tm, pl.cdiv(ff, tf)),
        in_specs=[
            pl.BlockSpec((tm, d), lambda i, f: (i, 0)),
            pl.BlockSpec((1, d), lambda i, f: (0, 0)),
            pl.BlockSpec((SUBLANES, N_MOD * d), lambda i, f: (0, 0)),
            pl.BlockSpec((1, d, tf), lambda i, f: (layer, 0, f)),
            pl.BlockSpec((1, d, tf), lambda i, f: (layer, 0, f)),
            pl.BlockSpec((1, tf, d), lambda i, f: (layer, f, 0)),
        ],
        out_specs=pl.BlockSpec((tm, d), lambda i, f: (i, 0)),
        out_shape=jax.ShapeDtypeStruct((t, d), F32),
        scratch_shapes=[pltpu.VMEM((tm, d), BF16), pltpu.VMEM((tm, d), F32)],
        compiler_params=pltpu.CompilerParams(
            dimension_semantics=("arbitrary", "arbitrary"),
            vmem_limit_bytes=VMEM_LIMIT),
        name="dense_ffn",
    )(x2, g.reshape(1, d), mod, w1, w3, w2)


def _sincos_2d(rows, cols, dim):
    quarter = dim // 4
    omega = 1.0 / (10000.0 ** (jnp.arange(quarter, dtype=F32) / quarter))

    def emb1d(n):
        ang = jnp.arange(n, dtype=F32)[:, None] * omega[None, :]
        return jnp.concatenate([jnp.sin(ang), jnp.cos(ang)], axis=-1)

    er = jnp.broadcast_to(emb1d(rows)[:, None, :], (rows, cols, dim // 2))
    ec = jnp.broadcast_to(emb1d(cols)[None, :, :], (rows, cols, dim // 2))
    return jnp.concatenate([er, ec], axis=-1).reshape(rows * cols, dim)


def _block_diag_tiles(w):
    per = MXU_DIM // LRU_HEAD_DIM
    w4 = w.reshape(D_LRU // MXU_DIM, per, LRU_HEAD_DIM, LRU_HEAD_DIM)
    eye = jnp.eye(per, dtype=w.dtype)
    return jnp.einsum("ghij,hk->ghikj", w4, eye).reshape(D_LRU // MXU_DIM, MXU_DIM, MXU_DIM)


def _gate_weights(wa, wx):
    return jnp.concatenate([_block_diag_tiles(wa), _block_diag_tiles(wx)], axis=-1).astype(BF16)


def _lru_vectors(conv_w, conv_b, ba, bx, lam):
    rows = jnp.concatenate([conv_w, conv_b[None], ba[None], bx[None], lam[None]], axis=0)
    return jnp.pad(rows, ((0, 2 * SUBLANES - rows.shape[0]), (0, 0)))


def kernel(x, c, ctx, c_ctx, w_mod, b_mod, norm1_g, norm2_g, w_in, w_out, lru_conv_w, lru_conv_b, lru_wa, lru_ba, lru_wx, lru_bx, lru_lambda, conf_dw_w, conf_dw_b, conf_ln_g, conf_ln_b, ffn_w1, ffn_w3, ffn_w2, moe_router, moe_w1, moe_w3, moe_w2, final_g):
    nb, s, d = x.shape
    n_ctx = ctx.shape[1]
    pos = _sincos_2d(s // GRID_W, GRID_W, d)
    cc = jnp.concatenate([c, c_ctx[None], jnp.zeros((SUBLANES - nb - 1, d), F32)], axis=0)
    mod_all = _modulation(cc, w_mod, b_mod)

    ts_x, ts_c = 512, n_ctx
    tm_x = 1024
    tiles_per_batch = s // tm_x
    zero_state = jnp.zeros((nb, SUBLANES, D_LRU), F32)
    xc = ctx.reshape(1, nb * n_ctx, d)

    ffn_w1_b, ffn_w3_b, ffn_w2_b = (w.astype(BF16) for w in (ffn_w1, ffn_w3, ffn_w2))

    for l in range(DEPTH):
        last = l == DEPTH - 1
        mod = mod_all[l]
        w_in_b = w_in[l].astype(BF16)
        w_out_b = w_out[l].astype(BF16)
        pvec = [_lru_vectors(lru_conv_w[l], lru_conv_b[l], lru_ba[l, dr], lru_bx[l, dr],
                             lru_lambda[l, dr]) for dr in range(2)]
        wg = [_gate_weights(lru_wa[l, dr], lru_wx[l, dr]) for dr in range(2)]
        dww = jnp.repeat(conf_dw_w[l], SUBLANES, axis=0)
        cvec = jnp.pad(jnp.stack([conf_dw_b[l], conf_ln_g[l], conf_ln_b[l]]),
                       ((0, SUBLANES - 3), (0, 0)))

        u_c, hf_c, rc_c, st_f, _ = _in_forward(
            xc.reshape(nb, n_ctx, d), None, norm1_g[l], mod, w_in_b[:, :D_LRU] if last else w_in_b,
            pvec[0], wg[0], zero_state, ts=ts_c, mod_row=CTX_MOD_ROW)
        u_x, hf_x, rc_x, _, x = _in_forward(
            x, pos if l == 0 else None, norm1_g[l], mod, w_in_b, pvec[0], wg[0], st_f,
            ts=ts_x, mod_row=None)
        if last:
            _, st_b = _lru_backward(rc_c, pvec[1], wg[1], zero_state, ts=ts_c)
        else:
            xc_new, st_b = _lru_backward(
                rc_c, pvec[1], wg[1], zero_state, ts=ts_c,
                merge_args=(u_c, hf_c, xc.reshape(nb, n_ctx, d), mod, dww, cvec, w_out_b),
                mod_row=CTX_MOD_ROW)
            xc = xc_new.reshape(1, nb * n_ctx, d)
        x, _ = _lru_backward(rc_x, pvec[1], wg[1], st_b, ts=ts_x,
                             merge_args=(u_x, hf_x, x, mod, dww, cvec, w_out_b), mod_row=None)

        j = l // 2
        x2 = x.reshape(nb * s, d)
        if l % 2 == 0:
            x2 = _ffn(x2, norm2_g[l], mod, ffn_w1_b, ffn_w3_b, ffn_w2_b, j, tm=tm_x,
                      tiles_per_batch=tiles_per_batch, mod_row=None)
            xc2 = _ffn(xc[0], norm2_g[l], mod, ffn_w1_b, ffn_w3_b, ffn_w2_b, j, tm=512,
                       tiles_per_batch=1, mod_row=CTX_MOD_ROW)
            xc = xc2.reshape(1, nb * n_ctx, d)
        else:
            groups = [(x2, None)] if last else [(x2, None), (xc[0], CTX_MOD_ROW)]
            blocks_per_batch = s // ROUTE_BLOCK
            wr = jnp.pad(moe_router[j], ((0, 0), (0, LANES - N_EXPERTS)))
            routed = [_route(xg, norm2_g[l], mod, wr, mod_row=row, blocks_per_batch=blocks_per_batch)
                      for xg, row in groups]
            cnt = jnp.concatenate([r[2] for r in routed], axis=0)
            nblk = cnt.shape[0]
            n_tiles_max = (nblk * CHUNKS_PER_BLOCK + N_EXPERTS * (CHUNKS_PER_TILE - 1)) // CHUNKS_PER_TILE
            tables = _route_tables(cnt, n_tiles_max)
            ys = _moe_grouped([r[0] for r in routed], [r[1] for r in routed], tables,
                              moe_w1, moe_w3, moe_w2, j, n_tiles_max=n_tiles_max)
            outs, first = [], 0
            for (xg, row), r in zip(groups, routed):
                n_chunks = xg.shape[0] // ROUTE_BLOCK * CHUNKS_PER_BLOCK
                outs.append(_combine(ys, tables[3][first:first + n_chunks], r[3], xg, final_g, mod,
                                     mod_row=row, blocks_per_batch=blocks_per_batch, final=last))
                first += n_chunks
            x2 = outs[0]
            if not last:
                xc = outs[1].reshape(1, nb * n_ctx, d)
        x = x2.reshape(nb, s, d)
    return x
```

```python
import functools
import math

import jax
import jax.numpy as jnp
from jax import lax
from jax.experimental import pallas as pl
from jax.experimental.pallas import tpu as pltpu

F32 = jnp.float32
BF16 = jnp.bfloat16

D_MODEL = 1024
DEPTH = 4
GRID_W = 64
D_LRU = 512
D_CONV = 512
LRU_HEADS = 8
LRU_HEAD_DIM = 64
LRU_CONV_W = 4
LRU_C = 8.0
CONF_CONV_W = 31
D_FF = 2816
N_EXPERTS = 8
N_MOD = 6
EPS = 1e-6

LANES = 128
SUBLANES = 8
BF16_ROWS = 16
MXU_DIM = 256
VMEM_LIMIT = 60 * 1024 * 1024

HALO = BF16_ROWS
CONV_CHUNK = 32
CTX_MOD_ROW = 4


def _sigmoid(v):
    return 0.5 * jnp.tanh(0.5 * v) + 0.5


def _silu(v):
    h = 0.5 * v
    return h + h * jnp.tanh(h)


def _rms_mod(x, g, shift, scale):
    ms = jnp.mean(x * x, axis=-1, keepdims=True)
    y = x * lax.rsqrt(ms + EPS) * g
    return y * (1.0 + scale) + shift


def _mod_row(mod_ref, row, k):
    r = mod_ref[pl.ds(row, 1), :]
    return r[:, k * D_MODEL:(k + 1) * D_MODEL]


def _mod_kernel(cc_ref, w_ref, b_ref, o_ref):
    s = _silu(cc_ref[...])
    o_ref[0] = jnp.dot(s.astype(BF16), w_ref[0].astype(BF16),
                       preferred_element_type=F32) + b_ref[0]


def _modulation(cc, w_mod, b_mod):
    n_out = N_MOD * D_MODEL
    tn = 1536
    return pl.pallas_call(
        _mod_kernel,
        grid=(DEPTH, n_out // tn),
        in_specs=[
            pl.BlockSpec((SUBLANES, D_MODEL), lambda l, j: (0, 0)),
            pl.BlockSpec((1, D_MODEL, tn), lambda l, j: (l, 0, j)),
            pl.BlockSpec((1, 1, tn), lambda l, j: (l, 0, j)),
        ],
        out_specs=pl.BlockSpec((1, SUBLANES, tn), lambda l, j: (l, 0, j)),
        out_shape=jax.ShapeDtypeStruct((DEPTH, SUBLANES, n_out), F32),
        compiler_params=pltpu.CompilerParams(
            dimension_semantics=("arbitrary", "arbitrary"),
            vmem_limit_bytes=VMEM_LIMIT),
        name="modulation",
    )(cc, w_mod, b_mod.reshape(DEPTH, 1, n_out))


def _short_conv(prev, main, nxt, pvec_ref, ts):
    groups = ts // SUBLANES
    ext = jnp.concatenate([prev, main, nxt], axis=0)
    x3 = ext.reshape(groups + 2, SUBLANES, D_LRU)
    row = lax.broadcasted_iota(jnp.int32, (groups, SUBLANES, D_LRU), 1)

    def window(off):
        if off == 0:
            return x3[1:groups + 1]
        r = pltpu.roll(x3, (-off) % SUBLANES, axis=1)
        if off < 0:
            return jnp.where(row >= -off, r[1:groups + 1], r[0:groups])
        return jnp.where(row < SUBLANES - off, r[1:groups + 1], r[2:groups + 2])

    rc = jnp.zeros((groups, SUBLANES, D_LRU), F32) + pvec_ref[4:5, :]
    for k in range(LRU_CONV_W):
        rc = rc + pvec_ref[k:k + 1, :] * window(k - 2)
    return rc.reshape(ts, D_LRU)


def _lru_gates(rc, pvec_ref, wg_ref):
    rcb = rc.astype(BF16)
    ga, gx = [], []
    for j in range(D_LRU // MXU_DIM):
        o = jnp.dot(rcb[:, j * MXU_DIM:(j + 1) * MXU_DIM], wg_ref[j],
                    preferred_element_type=F32)
        ga.append(o[:, :MXU_DIM])
        gx.append(o[:, MXU_DIM:])
    r_gate = _sigmoid(jnp.concatenate(ga, axis=-1) + pvec_ref[5:6, :])
    i_gate = _sigmoid(jnp.concatenate(gx, axis=-1) + pvec_ref[6:7, :])
    neg_lam = -pvec_ref[7:8, :]
    softplus = jnp.maximum(neg_lam, 0.0) + jnp.log1p(jnp.exp(-jnp.abs(neg_lam)))
    log_a = -LRU_C * r_gate * softplus
    a = jnp.exp(log_a)
    one_minus_a2 = -jnp.tanh(log_a) * (a * a + 1.0)
    b = jnp.sqrt(one_minus_a2) * (i_gate * rc)
    return a, b


def _lru_scan(a, b, h_in, hs_ref, ts, reverse):
    groups = ts // SUBLANES
    c = a.shape[-1]
    a3 = a.reshape(groups, SUBLANES, c)
    b3 = b.reshape(groups, SUBLANES, c)
    row = lax.broadcasted_iota(jnp.int32, (groups, SUBLANES, c), 1)
    for s in (1, 2, 4):
        shift = SUBLANES - s if reverse else s
        a_sh = pltpu.roll(a3, shift, axis=1)
        b_sh = pltpu.roll(b3, shift, axis=1)
        m = (row < SUBLANES - s) if reverse else (row >= s)
        b3 = jnp.where(m, a3 * b_sh + b3, b3)
        a3 = jnp.where(m, a3 * a_sh, a3)
    h = h_in
    order = range(groups - 1, -1, -1) if reverse else range(groups)
    edge = 0 if reverse else SUBLANES - 1
    for g in order:
        hg = a3[g] * h + b3[g]
        hs_ref[g * SUBLANES:(g + 1) * SUBLANES, :] = hg
        h = jnp.broadcast_to(hg[edge:edge + 1, :], (SUBLANES, c))
    return h


def _halo_specs(width, col, ts, s, tile_of):
    per = ts // HALO
    last = s // HALO - 1
    main = pl.BlockSpec((1, ts, width), lambda b, t: (b, tile_of(t), col))
    prev = pl.BlockSpec((1, HALO, width),
                        lambda b, t: (b, jnp.maximum(tile_of(t) * per - 1, 0), col))
    nxt = pl.BlockSpec((1, HALO, width),
                       lambda b, t: (b, jnp.minimum((tile_of(t) + 1) * per, last), col))
    return [main, prev, nxt]


def _in_fwd_kernel(*refs, ts, add_pos, mod_row):
    if add_pos:
        (x_ref, xp_ref, xn_ref, pos_ref, posp_ref, posn_ref, g_ref, mod_ref, w_ref, pvec_ref, wg_ref,
         seed_ref, u_ref, hf_ref, rc_ref, state_ref, xo_ref, hs_ref, carry_ref) = refs
    else:
        (x_ref, xp_ref, xn_ref, g_ref, mod_ref, w_ref, pvec_ref, wg_ref,
         seed_ref, u_ref, hf_ref, rc_ref, state_ref, hs_ref, carry_ref) = refs
    t = pl.program_id(1)
    nt = pl.num_programs(1)
    row = pl.program_id(0) if mod_row is None else mod_row

    @pl.when(t == 0)
    def _():
        carry_ref[...] = seed_ref[0]

    g = g_ref[...]
    shift = _mod_row(mod_ref, row, 0)
    scale = _mod_row(mod_ref, row, 1)
    x, xp, xn = x_ref[0], xp_ref[0], xn_ref[0]
    if add_pos:
        x, xp, xn = x + pos_ref[...], xp + posp_ref[...], xn + posn_ref[...]
        xo_ref[0] = x
    u = jnp.dot(_rms_mod(x, g, shift, scale).astype(BF16), w_ref[...], preferred_element_type=F32)
    u_ref[0] = u.astype(BF16)

    def halo_rows(xh, keep):
        h = _rms_mod(xh, g, shift, scale).astype(BF16)
        return jnp.dot(h, w_ref[:, 0:D_LRU], preferred_element_type=F32) * keep

    prev = halo_rows(xp, (t > 0).astype(F32))[HALO - SUBLANES:HALO, :]
    nxt = halo_rows(xn, (t < nt - 1).astype(F32))[0:SUBLANES, :]
    rc = _short_conv(prev, u[:, 0:D_LRU], nxt, pvec_ref, ts)
    rc_ref[0] = rc
    a, b = _lru_gates(rc, pvec_ref, wg_ref)
    h = _lru_scan(a, b, carry_ref[...], hs_ref, ts, reverse=False)
    carry_ref[...] = h
    state_ref[0] = h
    hf_ref[0] = hs_ref[...].astype(BF16)


def _in_forward(x, pos, g, mod, w, pvec, wg, seed, *, ts, mod_row):
    nb, s, d = x.shape
    n = w.shape[1]
    add_pos = pos is not None
    per = ts // HALO
    last = s // HALO - 1

    def prev_blk(t):
        return jnp.maximum(t * per - 1, 0)

    def next_blk(t):
        return jnp.minimum((t + 1) * per, last)

    in_specs = [
        pl.BlockSpec((1, ts, d), lambda b, t: (b, t, 0)),
        pl.BlockSpec((1, HALO, d), lambda b, t: (b, prev_blk(t), 0)),
        pl.BlockSpec((1, HALO, d), lambda b, t: (b, next_blk(t), 0)),
    ]
    args = [x, x, x]
    if add_pos:
        in_specs += [
            pl.BlockSpec((ts, d), lambda b, t: (t, 0)),
            pl.BlockSpec((HALO, d), lambda b, t: (prev_blk(t), 0)),
            pl.BlockSpec((HALO, d), lambda b, t: (next_blk(t), 0)),
        ]
        args += [pos, pos, pos]
    in_specs += [
        pl.BlockSpec((1, d), lambda b, t: (0, 0)),
        pl.BlockSpec((SUBLANES, N_MOD * d), lambda b, t: (0, 0)),
        pl.BlockSpec((d, n), lambda b, t: (0, 0)),
        pl.BlockSpec((2 * SUBLANES, D_LRU), lambda b, t: (0, 0)),
        pl.BlockSpec((D_LRU // MXU_DIM, MXU_DIM, 2 * MXU_DIM), lambda b, t: (0, 0, 0)),
        pl.BlockSpec((1, SUBLANES, D_LRU), lambda b, t: (b, 0, 0)),
    ]
    args += [g.reshape(1, d), mod, w, pvec, wg, seed]
    out_specs = [
        pl.BlockSpec((1, ts, n), lambda b, t: (b, t, 0)),
        pl.BlockSpec((1, ts, D_LRU), lambda b, t: (b, t, 0)),
        pl.BlockSpec((1, ts, D_LRU), lambda b, t: (b, t, 0)),
        pl.BlockSpec((1, SUBLANES, D_LRU), lambda b, t: (b, 0, 0)),
    ]
    out_shape = [
        jax.ShapeDtypeStruct((nb, s, n), BF16),
        jax.ShapeDtypeStruct((nb, s, D_LRU), BF16),
        jax.ShapeDtypeStruct((nb, s, D_LRU), F32),
        jax.ShapeDtypeStruct((nb, SUBLANES, D_LRU), F32),
    ]
    if add_pos:
        out_specs.append(pl.BlockSpec((1, ts, d), lambda b, t: (b, t, 0)))
        out_shape.append(jax.ShapeDtypeStruct((nb, s, d), F32))
    res = pl.pallas_call(
        functools.partial(_in_fwd_kernel, ts=ts, add_pos=add_pos, mod_row=mod_row),
        grid=(nb, s // ts),
        in_specs=in_specs,
        out_specs=out_specs,
        out_shape=out_shape,
        scratch_shapes=[
            pltpu.VMEM((ts, D_LRU), F32),
            pltpu.VMEM((SUBLANES, D_LRU), F32),
        ],
        compiler_params=pltpu.CompilerParams(
            dimension_semantics=("arbitrary", "arbitrary"),
            vmem_limit_bytes=VMEM_LIMIT),
        name="in_forward",
    )(*args)
    u, hf, rc, state = res[:4]
    return u, hf, rc, state, (res[4] if add_pos else x)


def _bwd_kernel(*refs, ts, merge, mod_row):
    if merge:
        (rc_ref, pvec_ref, wg_ref, seed_ref,
         gate_ref, v_ref, vprev_ref, vnext_ref, hf_ref, x_ref, mod_ref,
         dww_ref, cvec_ref, wo_ref,
         xo_ref, state_ref,
         hs_ref, carry_ref, g_ref, cs_ref, sh_ref) = refs
    else:
        (rc_ref, pvec_ref, wg_ref, seed_ref,
         state_ref, hs_ref, carry_ref) = refs
    tg = pl.program_id(1)
    nt = pl.num_programs(1)
    t = nt - 1 - tg

    @pl.when(tg == 0)
    def _():
        carry_ref[...] = seed_ref[0]

    has_prev = (t > 0).astype(F32)
    has_next = (t < nt - 1).astype(F32)
    a, b = _lru_gates(rc_ref[0], pvec_ref, wg_ref)
    h = _lru_scan(a, b, carry_ref[...], hs_ref, ts, reverse=True)
    carry_ref[...] = h
    state_ref[0] = h
    if not merge:
        return

    y = hf_ref[0].astype(F32) + hs_ref[...]
    gate = gate_ref[0].astype(F32)
    cdf = 0.5 * (1.0 + jnp.tanh(math.sqrt(2.0 / math.pi) * (gate + 0.044715 * (gate * gate * gate))))
    lru = (y * (gate * cdf)).astype(BF16)

    def glu(vref):
        v = vref[0].astype(F32)
        return v[:, :D_CONV] * _sigmoid(v[:, D_CONV:])

    g_ref[0:HALO, :] = glu(vprev_ref) * has_prev
    g_ref[HALO:HALO + ts, :] = glu(v_ref)
    g_ref[HALO + ts:2 * HALO + ts, :] = glu(vnext_ref) * has_next
    pad = (CONF_CONV_W - 1) // 2
    dwb = cvec_ref[0:1, :]
    span = ts + 2 * HALO - SUBLANES
    for r in range(1, SUBLANES):
        sh_ref[r - 1, 0:span, :] = g_ref[pl.ds(r, span), :]

    def conv_chunk(i, carry):
        base = pl.multiple_of(i * CONV_CHUNK, CONV_CHUNK)
        groups = CONV_CHUNK // SUBLANES
        acc = jnp.zeros((groups, SUBLANES, D_CONV), F32) + dwb
        for k in range(CONF_CONV_W):
            q, r = divmod(HALO - pad + k, SUBLANES)
            rows = pl.ds(base + q * SUBLANES, CONV_CHUNK)
            tap = g_ref[rows, :] if r == 0 else sh_ref[r - 1, rows, :]
            wk = dww_ref[k * SUBLANES:(k + 1) * SUBLANES, :]
            acc = acc + wk[None] * tap.reshape(groups, SUBLANES, D_CONV)
        cs_ref[pl.ds(base, CONV_CHUNK), :] = acc.reshape(CONV_CHUNK, D_CONV)
        return carry

    lax.fori_loop(0, ts // CONV_CHUNK, conv_chunk, 0)
    cv = cs_ref[...]
    mu = jnp.mean(cv, axis=-1, keepdims=True)
    var = jnp.mean(jnp.square(cv - mu), axis=-1, keepdims=True)
    ln = (cv - mu) * lax.rsqrt(var + EPS) * cvec_ref[1:2, :] + cvec_ref[2:3, :]
    conf = _silu(ln).astype(BF16)

    o = jnp.dot(lru, wo_ref[0:D_LRU, :], preferred_element_type=F32)
    o = o + jnp.dot(conf, wo_ref[D_LRU:, :], preferred_element_type=F32)
    row = pl.program_id(0) if mod_row is None else mod_row
    xo_ref[0] = x_ref[0] + _mod_row(mod_ref, row, 2) * o


def _lru_backward(rc, pvec, wg, seed, *, ts, merge_args=None, mod_row=None):
    nb, s, _ = rc.shape
    nt = s // ts
    rev = lambda t: nt - 1 - t
    in_specs = [
        pl.BlockSpec((1, ts, D_LRU), lambda b, t: (b, rev(t), 0)),
        pl.BlockSpec((2 * SUBLANES, D_LRU), lambda b, t: (0, 0)),
        pl.BlockSpec((D_LRU // MXU_DIM, MXU_DIM, 2 * MXU_DIM), lambda b, t: (0, 0, 0)),
        pl.BlockSpec((1, SUBLANES, D_LRU), lambda b, t: (b, 0, 0)),
    ]
    args = [rc, pvec, wg, seed]
    state_spec = pl.BlockSpec((1, SUBLANES, D_LRU), lambda b, t: (b, 0, 0))
    state_shape = jax.ShapeDtypeStruct((nb, SUBLANES, D_LRU), F32)
    scratch = [
        pltpu.VMEM((ts, D_LRU), F32),
        pltpu.VMEM((SUBLANES, D_LRU), F32),
    ]
    merge = merge_args is not None
    if merge:
        u, hf, x, mod, dww, cvec, wo = merge_args
        d = x.shape[-1]
        in_specs += [pl.BlockSpec((1, ts, D_LRU), lambda b, t: (b, rev(t), 1))]
        in_specs += _halo_specs(2 * D_CONV, 1, ts, s, rev)
        in_specs += [
            pl.BlockSpec((1, ts, D_LRU), lambda b, t: (b, rev(t), 0)),
            pl.BlockSpec((1, ts, d), lambda b, t: (b, rev(t), 0)),
            pl.BlockSpec((SUBLANES, N_MOD * d), lambda b, t: (0, 0)),
            pl.BlockSpec((CONF_CONV_W * SUBLANES, D_CONV), lambda b, t: (0, 0)),
            pl.BlockSpec((SUBLANES, D_CONV), lambda b, t: (0, 0)),
            pl.BlockSpec((d, d), lambda b, t: (0, 0)),
        ]
        args += [u, u, u, u, hf, x, mod, dww, cvec, wo]
        out_specs = [pl.BlockSpec((1, ts, d), lambda b, t: (b, rev(t), 0)), state_spec]
        out_shape = [jax.ShapeDtypeStruct((nb, s, d), F32), state_shape]
        scratch += [
            pltpu.VMEM((ts + 2 * HALO, D_CONV), F32),
            pltpu.VMEM((ts, D_CONV), F32),
            pltpu.VMEM((SUBLANES - 1, ts + 2 * HALO, D_CONV), F32),
        ]
    else:
        out_specs = [state_spec]
        out_shape = [state_shape]
    res = pl.pallas_call(
        functools.partial(_bwd_kernel, ts=ts, merge=merge, mod_row=mod_row),
        grid=(nb, nt),
        in_specs=in_specs,
        out_specs=out_specs,
        out_shape=out_shape,
        scratch_shapes=scratch,
        compiler_params=pltpu.CompilerParams(
            dimension_semantics=("arbitrary", "arbitrary"),
            vmem_limit_bytes=VMEM_LIMIT),
        name="lru_backward_merge" if merge else "lru_backward",
    )(*args)
    return res if merge else (None, res[0])


ROUTE_BLOCK = 512
CHUNK = BF16_ROWS
BLOCK_CAP = 2 * ROUTE_BLOCK + LANES
CHUNKS_PER_BLOCK = BLOCK_CAP // CHUNK
MOE_TILE = 1024
FF_BLOCK = 6 * MXU_DIM
FF_SUB = MXU_DIM
CHUNKS_PER_TILE = MOE_TILE // CHUNK


def _split3(p):
    hi = p.astype(BF16).astype(F32)
    mid = (p - hi).astype(BF16).astype(F32)
    lo = ((p - hi) - mid).astype(BF16).astype(F32)
    return hi, mid, lo


def _route_kernel(*refs, blocks_per_batch, n_lat_blocks):
    has_ctx = len(refs) == 8
    if has_ctx:
        x_ref, xc_ref, g_ref, mod_ref, wr_ref, hs_ref, cnt_ref, pos_ref = refs
    else:
        x_ref, g_ref, mod_ref, wr_ref, hs_ref, cnt_ref, pos_ref = refs
    i = pl.program_id(0)
    x = x_ref[...]
    row = i // blocks_per_batch
    if has_ctx:
        is_lat = i < n_lat_blocks
        x = jnp.where(is_lat, x, xc_ref[...])
        row = jnp.where(is_lat, row, CTX_MOD_ROW)
    h = _rms_mod(x, g_ref[...], _mod_row(mod_ref, row, 3), _mod_row(mod_ref, row, 4))
    wr = wr_ref[...]
    h_hi = h.astype(BF16)
    h_lo = (h - h_hi.astype(F32)).astype(BF16)
    w_hi = wr.astype(BF16)
    w_lo = (wr - w_hi.astype(F32)).astype(BF16)
    logits = (jnp.dot(h_hi, w_hi, preferred_element_type=F32)
              + jnp.dot(h_hi, w_lo, preferred_element_type=F32)
              + jnp.dot(h_lo, w_hi, preferred_element_type=F32))
    lane = lax.broadcasted_iota(jnp.int32, logits.shape, 1).astype(F32)
    neg = jnp.float32(-jnp.inf)
    l1 = jnp.where(lane < N_EXPERTS, logits, neg)
    m1 = jnp.max(l1, axis=-1, keepdims=True)
    i1 = jnp.min(jnp.where(l1 == m1, lane, float(LANES)), axis=-1, keepdims=True)
    l2 = jnp.where(lane == i1, neg, l1)
    m2 = jnp.max(l2, axis=-1, keepdims=True)
    i2 = jnp.min(jnp.where(l2 == m2, lane, float(LANES)), axis=-1, keepdims=True)
    e = jnp.exp(m2 - m1)
    p1 = 1.0 / (1.0 + e)
    p2 = e / (1.0 + e)

    sel1 = lane == i1
    sel2 = lane == i2
    member = jnp.where(sel1 | sel2, 1.0, 0.0)
    cnt = jnp.sum(member, axis=0, keepdims=True)
    padded = jnp.floor((cnt + (CHUNK - 1.0)) * (1.0 / CHUNK)) * CHUNK
    r128 = lax.broadcasted_iota(jnp.int32, (LANES, LANES), 0)
    c128 = lax.broadcasted_iota(jnp.int32, (LANES, LANES), 1)
    before = jnp.where(r128 < c128, 1.0, 0.0)
    seg_off = jnp.dot(jnp.broadcast_to(padded, (SUBLANES, LANES)), before,
                      preferred_element_type=F32, precision=lax.Precision.HIGHEST)[0:1, :]
    bt = member.shape[0]
    rt = lax.broadcasted_iota(jnp.int32, (bt, bt), 0)
    ct = lax.broadcasted_iota(jnp.int32, (bt, bt), 1)
    earlier = jnp.where(ct < rt, 1.0, 0.0).astype(BF16)
    rank = jnp.dot(earlier, member.astype(BF16), preferred_element_type=F32)
    slot_of = seg_off + rank
    pos1 = jnp.sum(jnp.where(sel1, slot_of, 0.0), axis=-1, keepdims=True)
    pos2 = jnp.sum(jnp.where(sel2, slot_of, 0.0), axis=-1, keepdims=True)
    stats = jnp.where(lane == 0.0, pos1, 0.0) + jnp.where(lane == 1.0, pos2, 0.0)
    pos_ref[...] = stats
    cnt_ref[0] = jnp.broadcast_to(cnt, (SUBLANES, LANES))

    st = stats.T
    slot = lax.broadcasted_iota(jnp.int32, (BLOCK_CAP, bt), 0).astype(F32)
    hit1 = jnp.where(slot == st[0:1, :], 1.0, 0.0).astype(BF16)
    hit2 = jnp.where(slot == st[1:2, :], 1.0, 0.0).astype(BF16)
    d = h.shape[1]
    hs_ref[:, 0:d] = jnp.dot(hit1 + hit2, h_hi, preferred_element_type=F32).astype(BF16)
    parts1 = sum(jnp.where(lane == float(k), v, 0.0) for k, v in enumerate(_split3(p1)))
    parts2 = sum(jnp.where(lane == float(k), v, 0.0) for k, v in enumerate(_split3(p2)))
    wrow = (jnp.dot(hit1, parts1.astype(BF16), preferred_element_type=F32)
            + jnp.dot(hit2, parts2.astype(BF16), preferred_element_type=F32))
    hs_ref[:, d:] = wrow.astype(BF16)


def _route(x_lat, x_ctx, g, mod, wr, *, blocks_per_batch):
    d = x_lat.shape[1]
    n_lat = x_lat.shape[0] // ROUTE_BLOCK
    n_ctx = 0 if x_ctx is None else x_ctx.shape[0] // ROUTE_BLOCK
    nblk = n_lat + n_ctx
    in_specs = [pl.BlockSpec((ROUTE_BLOCK, d), lambda i: (jnp.minimum(i, n_lat - 1), 0))]
    args = [x_lat]
    if n_ctx:
        in_specs.append(pl.BlockSpec((ROUTE_BLOCK, d), lambda i: (jnp.clip(i - n_lat, 0, n_ctx - 1), 0)))
        args.append(x_ctx)
    in_specs += [
        pl.BlockSpec((1, d), lambda i: (0, 0)),
        pl.BlockSpec((SUBLANES, N_MOD * d), lambda i: (0, 0)),
        pl.BlockSpec((d, LANES), lambda i: (0, 0)),
    ]
    args += [g.reshape(1, d), mod, wr]
    return pl.pallas_call(
        functools.partial(_route_kernel, blocks_per_batch=blocks_per_batch, n_lat_blocks=n_lat),
        grid=(nblk,),
        in_specs=in_specs,
        out_specs=[
            pl.BlockSpec((BLOCK_CAP, d + LANES), lambda i: (i, 0)),
            pl.BlockSpec((1, SUBLANES, LANES), lambda i: (i, 0, 0)),
            pl.BlockSpec((ROUTE_BLOCK, LANES), lambda i: (i, 0)),
        ],
        out_shape=[
            jax.ShapeDtypeStruct((nblk * BLOCK_CAP, d + LANES), BF16),
            jax.ShapeDtypeStruct((nblk, SUBLANES, LANES), F32),
            jax.ShapeDtypeStruct((nblk * ROUTE_BLOCK, LANES), F32),
        ],
        compiler_params=pltpu.CompilerParams(
            dimension_semantics=("arbitrary",),
            vmem_limit_bytes=VMEM_LIMIT),
        name="route",
    )(*args)


def _route_tables(cnt, n_tiles_max):
    nblk = cnt.shape[0]
    n_be = (cnt[:, 0, :N_EXPERTS].astype(jnp.int32) + (CHUNK - 1)) // CHUNK
    ends_be = jnp.cumsum(n_be, axis=1)
    off_be = ends_be - n_be
    start_be = jnp.cumsum(n_be, axis=0) - n_be
    g_e = jnp.sum(n_be, axis=0)
    tiles_e = (g_e + (CHUNKS_PER_TILE - 1)) // CHUNKS_PER_TILE
    tend_e = jnp.cumsum(tiles_e)
    tstart_e = tend_e - tiles_e
    n_tiles = tend_e[-1]

    experts = jnp.arange(N_EXPERTS, dtype=jnp.int32)

    def pick(onehot, table):
        return jnp.sum(jnp.where(onehot, table, 0), axis=-1)

    c = jnp.arange(CHUNKS_PER_BLOCK, dtype=jnp.int32)
    e_bc = jnp.sum((c[None, :, None] >= ends_be[:, None, :]).astype(jnp.int32), axis=-1)
    oh_bc = e_bc[:, :, None] == experts
    shift_be = tstart_e[None, :] * CHUNKS_PER_TILE + start_be - off_be
    d_bc = pick(oh_bc, shift_be[:, None, :]) + c[None, :]
    dst = jnp.where(e_bc < N_EXPERTS, d_bc, -1).reshape(-1).astype(jnp.int32)

    ti = jnp.arange(n_tiles_max, dtype=jnp.int32)
    ti_cl = jnp.minimum(ti, n_tiles - 1)
    tile_expert = jnp.minimum(
        jnp.sum((ti_cl[:, None] >= tend_e[None, :]).astype(jnp.int32), axis=-1), N_EXPERTS - 1)
    oh_t = tile_expert[:, None] == experts
    first_q = (ti - pick(oh_t, tstart_e[None, :])) * CHUNKS_PER_TILE

    q = first_q[:, None] + jnp.arange(CHUNKS_PER_TILE, dtype=jnp.int32)[None, :]
    cum_tb = pick(oh_t[:, None, :], (start_be + n_be)[None, :, :])
    b_tj = jnp.minimum(jnp.sum((q[:, :, None] >= cum_tb[:, None, :]).astype(jnp.int32), axis=-1),
                       nblk - 1)
    oh_b = b_tj[:, :, None] == jnp.arange(nblk, dtype=jnp.int32)
    local_tb = pick(oh_t[:, None, :], (off_be - start_be)[None, :, :])
    src = b_tj * CHUNKS_PER_BLOCK + q + pick(oh_b, local_tb[:, None, :])
    src = jnp.clip(src, 0, nblk * CHUNKS_PER_BLOCK - 1).reshape(-1).astype(jnp.int32)
    return tile_expert.astype(jnp.int32), src, n_tiles.reshape(1).astype(jnp.int32), dst


DMA_UNROLL = 8


def _start_chunk_gather(table_ref, base, src_hbm, dst_buf, sem):
    n_slots = dst_buf.shape[0] // CHUNK

    def step(j, carry):
        cid = jnp.maximum(table_ref[base + j], 0)
        pltpu.make_async_copy(
            src_hbm.at[pl.ds(pl.multiple_of(cid * CHUNK, CHUNK), CHUNK), :],
            dst_buf.at[pl.ds(pl.multiple_of(j * CHUNK, CHUNK), CHUNK), :], sem).start()
        return carry

    lax.fori_loop(0, n_slots, step, 0, unroll=DMA_UNROLL)


def _wait_chunk_gather(src_hbm, dst_buf, sem):
    pltpu.make_async_copy(src_hbm.at[pl.ds(0, dst_buf.shape[0]), :], dst_buf, sem).wait()


def _swiglu_accumulate(load_x, w1_ref, w3_ref, w2_ref, acc_ref, f, ff):
    last_valid = (ff - (pl.cdiv(ff, FF_BLOCK) - 1) * FF_BLOCK) // FF_SUB
    is_last = f == pl.cdiv(ff, FF_BLOCK) - 1
    for c in range(FF_BLOCK // FF_SUB):
        cols = slice(c * FF_SUB, (c + 1) * FF_SUB)

        def chunk(cols=cols):
            xb = load_x()
            up = jnp.dot(xb, w1_ref[:, cols].astype(BF16), preferred_element_type=F32)
            lin = jnp.dot(xb, w3_ref[:, cols].astype(BF16), preferred_element_type=F32)
            hid = (_silu(up) * lin).astype(BF16)
            acc_ref[...] += jnp.dot(hid, w2_ref[cols, :].astype(BF16), preferred_element_type=F32)

        if c < last_valid:
            chunk()
        else:
            pl.when(jnp.logical_not(is_last))(chunk)


def _moe_kernel(te_ref, src_ref, nt_ref, hs_hbm, w1_ref, w3_ref, w2_ref, ys_ref, xbuf, acc_ref, sems,
                *, ff):
    i = pl.program_id(0)
    f = pl.program_id(1)
    n_tiles = nt_ref[0]
    valid = i < n_tiles
    slot = lax.rem(i, 2)
    d = acc_ref.shape[1]

    def start_gather(tile, buf_slot):
        _start_chunk_gather(src_ref, tile * CHUNKS_PER_TILE, hs_hbm, xbuf.at[buf_slot], sems.at[buf_slot])

    @pl.when((i == 0) & (f == 0))
    def _():
        start_gather(0, 0)

    @pl.when(valid & (f == 0))
    def _():
        _wait_chunk_gather(hs_hbm, xbuf.at[slot], sems.at[slot])

        @pl.when(i + 1 < n_tiles)
        def _():
            start_gather(i + 1, 1 - slot)

        acc_ref[...] = jnp.zeros_like(acc_ref)

    @pl.when(valid)
    def _():
        _swiglu_accumulate(lambda: xbuf[slot, :, 0:d], w1_ref.at[0, 0], w3_ref.at[0, 0], w2_ref.at[0, 0],
                           acc_ref, f, ff)

    @pl.when(valid & (f == pl.num_programs(1) - 1))
    def _():
        parts = xbuf[slot, :, d:d + LANES].astype(F32)
        w = parts[:, 0:1] + parts[:, 1:2] + parts[:, 2:3]
        ys_ref[...] = (acc_ref[...] * w).astype(BF16)

    @pl.when(jnp.logical_not(valid) & (f == 0))
    def _():
        ys_ref[...] = jnp.zeros_like(ys_ref)


def _moe_grouped(hs, tables, w1, w3, w2, layer, *, n_tiles_max):
    tile_expert, src, n_tiles, _ = tables
    d = hs.shape[1] - LANES
    ff = w1.shape[-1]
    tf = FF_BLOCK
    nf = pl.cdiv(ff, tf)

    def w_in_map(i, f, te, sr, nt):
        return (layer, te[i], 0, jnp.where(i < nt[0], f, nf - 1))

    def w_out_map(i, f, te, sr, nt):
        return (layer, te[i], jnp.where(i < nt[0], f, nf - 1), 0)

    def y_map(i, f, te, sr, nt):
        return (i, 0)

    grid_spec = pltpu.PrefetchScalarGridSpec(
        num_scalar_prefetch=3,
        grid=(n_tiles_max, nf),
        in_specs=[
            pl.BlockSpec(memory_space=pl.ANY),
            pl.BlockSpec((1, 1, d, tf), w_in_map),
            pl.BlockSpec((1, 1, d, tf), w_in_map),
            pl.BlockSpec((1, 1, tf, d), w_out_map),
        ],
        out_specs=pl.BlockSpec((MOE_TILE, d), y_map),
        scratch_shapes=[
            pltpu.VMEM((2, MOE_TILE, d + LANES), BF16),
            pltpu.VMEM((MOE_TILE, d), F32),
            pltpu.SemaphoreType.DMA((2,)),
        ],
    )
    return pl.pallas_call(
        functools.partial(_moe_kernel, ff=ff),
        grid_spec=grid_spec,
        out_shape=jax.ShapeDtypeStruct((n_tiles_max * MOE_TILE, d), BF16),
        compiler_params=pltpu.CompilerParams(
            dimension_semantics=("arbitrary", "arbitrary"),
            vmem_limit_bytes=VMEM_LIMIT),
        name="moe_grouped",
    )(tile_expert, src, n_tiles, hs, w1, w3, w2)


def _combine_kernel(dst_ref, ys_hbm, pos_ref, x_ref, fg_ref, mod_ref, o_ref, ybuf, sems,
                    *, blocks_per_batch, mod_row, first_block, final):
    i = pl.program_id(0)
    row = i // blocks_per_batch if mod_row is None else mod_row
    buf = lax.rem(i, 2)
    source = ys_hbm

    def start_gather(blk, buf_slot):
        _start_chunk_gather(dst_ref, (first_block + blk) * CHUNKS_PER_BLOCK, source,
                            ybuf.at[buf_slot], sems.at[buf_slot])

    @pl.when(i == 0)
    def _():
        start_gather(0, 0)

    @pl.when(i + 1 < pl.num_programs(0))
    def _():
        start_gather(i + 1, 1 - buf)

    stats = pos_ref[...]
    slot = lax.broadcasted_iota(jnp.int32, (stats.shape[0], BLOCK_CAP), 1).astype(F32)
    pick = jnp.where((slot == stats[:, 0:1]) | (slot == stats[:, 1:2]), 1.0, 0.0).astype(BF16)
    _wait_chunk_gather(source, ybuf.at[buf], sems.at[buf])
    moe = jnp.dot(pick, ybuf[buf], preferred_element_type=F32)
    y = x_ref[...] + _mod_row(mod_ref, row, 5) * moe
    if final:
        ms = jnp.mean(y * y, axis=-1, keepdims=True)
        y = y * lax.rsqrt(ms + EPS) * fg_ref[...]
    o_ref[...] = y


def _combine(ys, dst, pos, xall, fg, mod, *, first_block, mod_row, blocks_per_batch, final):
    t, d = xall.shape
    nblk = t // ROUTE_BLOCK
    grid_spec = pltpu.PrefetchScalarGridSpec(
        num_scalar_prefetch=1,
        grid=(nblk,),
        in_specs=[
            pl.BlockSpec(memory_space=pl.ANY),
            pl.BlockSpec((ROUTE_BLOCK, LANES), lambda i, ds: (first_block + i, 0)),
            pl.BlockSpec((ROUTE_BLOCK, d), lambda i, ds: (i, 0)),
            pl.BlockSpec((1, d), lambda i, ds: (0, 0)),
            pl.BlockSpec((SUBLANES, N_MOD * d), lambda i, ds: (0, 0)),
        ],
        out_specs=pl.BlockSpec((ROUTE_BLOCK, d), lambda i, ds: (i, 0)),
        scratch_shapes=[
            pltpu.VMEM((2, BLOCK_CAP, d), BF16),
            pltpu.SemaphoreType.DMA((2,)),
        ],
    )
    return pl.pallas_call(
        functools.partial(_combine_kernel, blocks_per_batch=blocks_per_batch, mod_row=mod_row,
                          first_block=first_block, final=final),
        grid_spec=grid_spec,
        out_shape=jax.ShapeDtypeStruct((t, d), F32),
        compiler_params=pltpu.CompilerParams(
            dimension_semantics=("arbitrary",),
            vmem_limit_bytes=VMEM_LIMIT),
        name="moe_combine",
    )(dst, ys, pos, xall, fg.reshape(1, d), mod)


def _ffn_kernel(x_ref, g_ref, mod_ref, w1_ref, w3_ref, w2_ref, o_ref, h_ref, acc_ref,
                *, tiles_per_batch, mod_row, ff):
    i = pl.program_id(0)
    f = pl.program_id(1)
    row = i // tiles_per_batch if mod_row is None else mod_row

    @pl.when(f == 0)
    def _():
        h = _rms_mod(x_ref[...], g_ref[...], _mod_row(mod_ref, row, 3), _mod_row(mod_ref, row, 4))
        h_ref[...] = h.astype(BF16)
        acc_ref[...] = jnp.zeros_like(acc_ref)

    _swiglu_accumulate(lambda: h_ref[...], w1_ref.at[0], w3_ref.at[0], w2_ref.at[0], acc_ref, f, ff)

    @pl.when(f == pl.num_programs(1) - 1)
    def _():
        o_ref[...] = x_ref[...] + _mod_row(mod_ref, row, 5) * acc_ref[...]


def _ffn(x2, g, mod, w1, w3, w2, layer, *, tm, tiles_per_batch, mod_row):
    t, d = x2.shape
    ff = w1.shape[-1]
    tf = FF_BLOCK
    return pl.pallas_call(
        functools.partial(_ffn_kernel, tiles_per_batch=tiles_per_batch, mod_row=mod_row, ff=ff),
        grid=(t // tm, pl.cdiv(ff, tf)),
        in_specs=[
            pl.BlockSpec((tm, d), lambda i, f: (i, 0)),
            pl.BlockSpec((1, d), lambda i, f: (0, 0)),
            pl.BlockSpec((SUBLANES, N_MOD * d), lambda i, f: (0, 0)),
            pl.BlockSpec((1, d, tf), lambda i, f: (layer, 0, f)),
            pl.BlockSpec((1, d, tf), lambda i, f: (layer, 0, f)),
            pl.BlockSpec((1, tf, d), lambda i, f: (layer, f, 0)),
        ],
        out_specs=pl.BlockSpec((tm, d), lambda i, f: (i, 0)),
        out_shape=jax.ShapeDtypeStruct((t, d), F32),
        scratch_shapes=[pltpu.VMEM((tm, d), BF16), pltpu.VMEM((tm, d), F32)],
        compiler_params=pltpu.CompilerParams(
            dimension_semantics=("arbitrary", "arbitrary"),
            vmem_limit_bytes=VMEM_LIMIT),
        name="dense_ffn",
    )(x2, g.reshape(1, d), mod, w1, w3, w2)


def _sincos_2d(rows, cols, dim):
    quarter = dim // 4
    omega = 1.0 / (10000.0 ** (jnp.arange(quarter, dtype=F32) / quarter))

    def emb1d(n):
        ang = jnp.arange(n, dtype=F32)[:, None] * omega[None, :]
        return jnp.concatenate([jnp.sin(ang), jnp.cos(ang)], axis=-1)

    er = jnp.broadcast_to(emb1d(rows)[:, None, :], (rows, cols, dim // 2))
    ec = jnp.broadcast_to(emb1d(cols)[None, :, :], (rows, cols, dim // 2))
    return jnp.concatenate([er, ec], axis=-1).reshape(rows * cols, dim)


def _block_diag_tiles(w):
    per = MXU_DIM // LRU_HEAD_DIM
    w4 = w.reshape(D_LRU // MXU_DIM, per, LRU_HEAD_DIM, LRU_HEAD_DIM)
    eye = jnp.eye(per, dtype=w.dtype)
    return jnp.einsum("ghij,hk->ghikj", w4, eye).reshape(D_LRU // MXU_DIM, MXU_DIM, MXU_DIM)


def _gate_weights(wa, wx):
    return jnp.concatenate([_block_diag_tiles(wa), _block_diag_tiles(wx)], axis=-1).astype(BF16)


def _lru_vectors(conv_w, conv_b, ba, bx, lam):
    rows = jnp.concatenate([conv_w, conv_b[None], ba[None], bx[None], lam[None]], axis=0)
    return jnp.pad(rows, ((0, 2 * SUBLANES - rows.shape[0]), (0, 0)))


def kernel(x, c, ctx, c_ctx, w_mod, b_mod, norm1_g, norm2_g, w_in, w_out, lru_conv_w, lru_conv_b, lru_wa, lru_ba, lru_wx, lru_bx, lru_lambda, conf_dw_w, conf_dw_b, conf_ln_g, conf_ln_b, ffn_w1, ffn_w3, ffn_w2, moe_router, moe_w1, moe_w3, moe_w2, final_g):
    nb, s, d = x.shape
    n_ctx = ctx.shape[1]
    pos = _sincos_2d(s // GRID_W, GRID_W, d)
    cc = jnp.concatenate([c, c_ctx[None], jnp.zeros((SUBLANES - nb - 1, d), F32)], axis=0)
    mod_all = _modulation(cc, w_mod, b_mod)

    ts_x, ts_c = 512, n_ctx
    tm_x = 1024
    tiles_per_batch = s // tm_x
    zero_state = jnp.zeros((nb, SUBLANES, D_LRU), F32)
    xc = ctx.reshape(1, nb * n_ctx, d)

    ffn_w1_b, ffn_w3_b, ffn_w2_b = (w.astype(BF16) for w in (ffn_w1, ffn_w3, ffn_w2))

    for l in range(DEPTH):
        last = l == DEPTH - 1
        mod = mod_all[l]
        w_in_b = w_in[l].astype(BF16)
        w_out_b = w_out[l].astype(BF16)
        pvec = [_lru_vectors(lru_conv_w[l], lru_conv_b[l], lru_ba[l, dr], lru_bx[l, dr],
                             lru_lambda[l, dr]) for dr in range(2)]
        wg = [_gate_weights(lru_wa[l, dr], lru_wx[l, dr]) for dr in range(2)]
        dww = jnp.repeat(conf_dw_w[l], SUBLANES, axis=0)
        cvec = jnp.pad(jnp.stack([conf_dw_b[l], conf_ln_g[l], conf_ln_b[l]]),
                       ((0, SUBLANES - 3), (0, 0)))

        u_c, hf_c, rc_c, st_f, _ = _in_forward(
            xc.reshape(nb, n_ctx, d), None, norm1_g[l], mod, w_in_b[:, :D_LRU] if last else w_in_b,
            pvec[0], wg[0], zero_state, ts=ts_c, mod_row=CTX_MOD_ROW)
        u_x, hf_x, rc_x, _, x = _in_forward(
            x, pos if l == 0 else None, norm1_g[l], mod, w_in_b, pvec[0], wg[0], st_f,
            ts=ts_x, mod_row=None)
        if last:
            _, st_b = _lru_backward(rc_c, pvec[1], wg[1], zero_state, ts=ts_c)
        else:
            xc_new, st_b = _lru_backward(
                rc_c, pvec[1], wg[1], zero_state, ts=ts_c,
                merge_args=(u_c, hf_c, xc.reshape(nb, n_ctx, d), mod, dww, cvec, w_out_b),
                mod_row=CTX_MOD_ROW)
            xc = xc_new.reshape(1, nb * n_ctx, d)
        x, _ = _lru_backward(rc_x, pvec[1], wg[1], st_b, ts=ts_x,
                             merge_args=(u_x, hf_x, x, mod, dww, cvec, w_out_b), mod_row=None)

        j = l // 2
        x2 = x.reshape(nb * s, d)
        if l % 2 == 0:
            x2 = _ffn(x2, norm2_g[l], mod, ffn_w1_b, ffn_w3_b, ffn_w2_b, j, tm=tm_x,
                      tiles_per_batch=tiles_per_batch, mod_row=None)
            xc2 = _ffn(xc[0], norm2_g[l], mod, ffn_w1_b, ffn_w3_b, ffn_w2_b, j, tm=512,
                       tiles_per_batch=1, mod_row=CTX_MOD_ROW)
            xc = xc2.reshape(1, nb * n_ctx, d)
        else:
            groups = [(x2, None)] if last else [(x2, None), (xc[0], CTX_MOD_ROW)]
            blocks_per_batch = s // ROUTE_BLOCK
            wr = jnp.pad(moe_router[j], ((0, 0), (0, LANES - N_EXPERTS)))
            hs, cnt, pos_tok = _route(x2, None if last else xc[0], norm2_g[l], mod, wr,
                                      blocks_per_batch=blocks_per_batch)
            nblk = cnt.shape[0]
            n_tiles_max = (nblk * CHUNKS_PER_BLOCK + N_EXPERTS * (CHUNKS_PER_TILE - 1)) // CHUNKS_PER_TILE
            tables = _route_tables(cnt, n_tiles_max)
            ys = _moe_grouped(hs, tables, moe_w1, moe_w3, moe_w2, j, n_tiles_max=n_tiles_max)
            outs, first = [], 0
            for xg, row in groups:
                outs.append(_combine(ys, tables[3], pos_tok, xg, final_g, mod, first_block=first,
                                     mod_row=row, blocks_per_batch=blocks_per_batch, final=last))
                first += xg.shape[0] // ROUTE_BLOCK
            x2 = outs[0]
            if not last:
                xc = outs[1].reshape(1, nb * n_ctx, d)
        x = x2.reshape(nb, s, d)
    return x
```

```python
import functools
import math

import jax
import jax.numpy as jnp
from jax import lax
from jax.experimental import pallas as pl
from jax.experimental.pallas import tpu as pltpu

F32 = jnp.float32
BF16 = jnp.bfloat16

D_MODEL = 1024
DEPTH = 4
GRID_W = 64
D_LRU = 512
D_CONV = 512
LRU_HEADS = 8
LRU_HEAD_DIM = 64
LRU_CONV_W = 4
LRU_C = 8.0
CONF_CONV_W = 31
D_FF = 2816
N_EXPERTS = 8
N_MOD = 6
EPS = 1e-6

LANES = 128
SUBLANES = 8
BF16_ROWS = 16
MXU_DIM = 256
VMEM_LIMIT = 60 * 1024 * 1024

HALO = BF16_ROWS
CONV_CHUNK = 64
CTX_MOD_ROW = 4


def _sigmoid(v):
    return 0.5 * jnp.tanh(0.5 * v) + 0.5


def _silu(v):
    h = 0.5 * v
    return h + h * jnp.tanh(h)


def _rms_mod(x, g, shift, scale):
    ms = jnp.mean(x * x, axis=-1, keepdims=True)
    y = x * lax.rsqrt(ms + EPS) * g
    return y * (1.0 + scale) + shift


def _mod_row(mod_ref, row, k):
    r = mod_ref[pl.ds(row, 1), :]
    return r[:, k * D_MODEL:(k + 1) * D_MODEL]


def _mod_kernel(cc_ref, w_ref, b_ref, o_ref):
    s = _silu(cc_ref[...])
    o_ref[0] = jnp.dot(s.astype(BF16), w_ref[0].astype(BF16),
                       preferred_element_type=F32) + b_ref[0]


def _modulation(cc, w_mod, b_mod):
    n_out = N_MOD * D_MODEL
    tn = 1536
    return pl.pallas_call(
        _mod_kernel,
        grid=(DEPTH, n_out // tn),
        in_specs=[
            pl.BlockSpec((SUBLANES, D_MODEL), lambda l, j: (0, 0)),
            pl.BlockSpec((1, D_MODEL, tn), lambda l, j: (l, 0, j)),
            pl.BlockSpec((1, 1, tn), lambda l, j: (l, 0, j)),
        ],
        out_specs=pl.BlockSpec((1, SUBLANES, tn), lambda l, j: (l, 0, j)),
        out_shape=jax.ShapeDtypeStruct((DEPTH, SUBLANES, n_out), F32),
        compiler_params=pltpu.CompilerParams(
            dimension_semantics=("arbitrary", "arbitrary"),
            vmem_limit_bytes=VMEM_LIMIT),
        name="modulation",
    )(cc, w_mod, b_mod.reshape(DEPTH, 1, n_out))


def _short_conv(prev, main, nxt, pvec_ref, ts):
    groups = ts // SUBLANES
    ext = jnp.concatenate([prev, main, nxt], axis=0)
    x3 = ext.reshape(groups + 2, SUBLANES, D_LRU)
    row = lax.broadcasted_iota(jnp.int32, (groups, SUBLANES, D_LRU), 1)

    def window(off):
        if off == 0:
            return x3[1:groups + 1]
        r = pltpu.roll(x3, (-off) % SUBLANES, axis=1)
        if off < 0:
            return jnp.where(row >= -off, r[1:groups + 1], r[0:groups])
        return jnp.where(row < SUBLANES - off, r[1:groups + 1], r[2:groups + 2])

    rc = jnp.zeros((groups, SUBLANES, D_LRU), F32) + pvec_ref[4:5, :]
    for k in range(LRU_CONV_W):
        rc = rc + pvec_ref[k:k + 1, :] * window(k - 2)
    return rc.reshape(ts, D_LRU)


def _lru_gates(rc, pvec_ref, wg_ref):
    rcb = rc.astype(BF16)
    ga, gx = [], []
    for j in range(D_LRU // MXU_DIM):
        o = jnp.dot(rcb[:, j * MXU_DIM:(j + 1) * MXU_DIM], wg_ref[j],
                    preferred_element_type=F32)
        ga.append(o[:, :MXU_DIM])
        gx.append(o[:, MXU_DIM:])
    r_gate = _sigmoid(jnp.concatenate(ga, axis=-1) + pvec_ref[5:6, :])
    i_gate = _sigmoid(jnp.concatenate(gx, axis=-1) + pvec_ref[6:7, :])
    neg_lam = -pvec_ref[7:8, :]
    softplus = jnp.maximum(neg_lam, 0.0) + jnp.log1p(jnp.exp(-jnp.abs(neg_lam)))
    log_a = -LRU_C * r_gate * softplus
    a = jnp.exp(log_a)
    one_minus_a2 = -jnp.tanh(log_a) * (a * a + 1.0)
    b = jnp.sqrt(one_minus_a2) * (i_gate * rc)
    return a, b


def _lru_scan(a, b, h_in, hs_ref, ts, reverse):
    groups = ts // SUBLANES
    c = a.shape[-1]
    a3 = a.reshape(groups, SUBLANES, c)
    b3 = b.reshape(groups, SUBLANES, c)
    row = lax.broadcasted_iota(jnp.int32, (groups, SUBLANES, c), 1)
    for s in (1, 2, 4):
        shift = SUBLANES - s if reverse else s
        a_sh = pltpu.roll(a3, shift, axis=1)
        b_sh = pltpu.roll(b3, shift, axis=1)
        m = (row < SUBLANES - s) if reverse else (row >= s)
        b3 = jnp.where(m, a3 * b_sh + b3, b3)
        a3 = jnp.where(m, a3 * a_sh, a3)
    h = h_in
    order = range(groups - 1, -1, -1) if reverse else range(groups)
    edge = 0 if reverse else SUBLANES - 1
    for g in order:
        hg = a3[g] * h + b3[g]
        hs_ref[g * SUBLANES:(g + 1) * SUBLANES, :] = hg
        h = jnp.broadcast_to(hg[edge:edge + 1, :], (SUBLANES, c))
    return h


def _halo_specs(width, col, ts, s, tile_of):
    per = ts // HALO
    last = s // HALO - 1
    main = pl.BlockSpec((1, ts, width), lambda b, t: (b, tile_of(t), col))
    prev = pl.BlockSpec((1, HALO, width),
                        lambda b, t: (b, jnp.maximum(tile_of(t) * per - 1, 0), col))
    nxt = pl.BlockSpec((1, HALO, width),
                       lambda b, t: (b, jnp.minimum((tile_of(t) + 1) * per, last), col))
    return [main, prev, nxt]


def _in_fwd_kernel(*refs, ts, add_pos, mod_row):
    if add_pos:
        (x_ref, xp_ref, xn_ref, pos_ref, posp_ref, posn_ref, g_ref, mod_ref, w_ref, pvec_ref, wg_ref,
         seed_ref, u_ref, hf_ref, rc_ref, state_ref, xo_ref, hs_ref, carry_ref) = refs
    else:
        (x_ref, xp_ref, xn_ref, g_ref, mod_ref, w_ref, pvec_ref, wg_ref,
         seed_ref, u_ref, hf_ref, rc_ref, state_ref, hs_ref, carry_ref) = refs
    t = pl.program_id(1)
    nt = pl.num_programs(1)
    row = pl.program_id(0) if mod_row is None else mod_row

    @pl.when(t == 0)
    def _():
        carry_ref[...] = seed_ref[0]

    g = g_ref[...]
    shift = _mod_row(mod_ref, row, 0)
    scale = _mod_row(mod_ref, row, 1)
    x, xp, xn = x_ref[0], xp_ref[0], xn_ref[0]
    if add_pos:
        x, xp, xn = x + pos_ref[...], xp + posp_ref[...], xn + posn_ref[...]
        xo_ref[0] = x
    u = jnp.dot(_rms_mod(x, g, shift, scale).astype(BF16), w_ref[...], preferred_element_type=F32)
    u_ref[0] = u.astype(BF16)

    def halo_rows(xh, keep):
        h = _rms_mod(xh, g, shift, scale).astype(BF16)
        return jnp.dot(h, w_ref[:, 0:D_LRU], preferred_element_type=F32) * keep

    prev = halo_rows(xp, (t > 0).astype(F32))[HALO - SUBLANES:HALO, :]
    nxt = halo_rows(xn, (t < nt - 1).astype(F32))[0:SUBLANES, :]
    rc = _short_conv(prev, u[:, 0:D_LRU], nxt, pvec_ref, ts)
    rc_ref[0] = rc
    a, b = _lru_gates(rc, pvec_ref, wg_ref)
    h = _lru_scan(a, b, carry_ref[...], hs_ref, ts, reverse=False)
    carry_ref[...] = h
    state_ref[0] = h
    hf_ref[0] = hs_ref[...].astype(BF16)


def _in_forward(x, pos, g, mod, w, pvec, wg, seed, *, ts, mod_row):
    nb, s, d = x.shape
    n = w.shape[1]
    add_pos = pos is not None
    per = ts // HALO
    last = s // HALO - 1

    def prev_blk(t):
        return jnp.maximum(t * per - 1, 0)

    def next_blk(t):
        return jnp.minimum((t + 1) * per, last)

    in_specs = [
        pl.BlockSpec((1, ts, d), lambda b, t: (b, t, 0)),
        pl.BlockSpec((1, HALO, d), lambda b, t: (b, prev_blk(t), 0)),
        pl.BlockSpec((1, HALO, d), lambda b, t: (b, next_blk(t), 0)),
    ]
    args = [x, x, x]
    if add_pos:
        in_specs += [
            pl.BlockSpec((ts, d), lambda b, t: (t, 0)),
            pl.BlockSpec((HALO, d), lambda b, t: (prev_blk(t), 0)),
            pl.BlockSpec((HALO, d), lambda b, t: (next_blk(t), 0)),
        ]
        args += [pos, pos, pos]
    in_specs += [
        pl.BlockSpec((1, d), lambda b, t: (0, 0)),
        pl.BlockSpec((SUBLANES, N_MOD * d), lambda b, t: (0, 0)),
        pl.BlockSpec((d, n), lambda b, t: (0, 0)),
        pl.BlockSpec((2 * SUBLANES, D_LRU), lambda b, t: (0, 0)),
        pl.BlockSpec((D_LRU // MXU_DIM, MXU_DIM, 2 * MXU_DIM), lambda b, t: (0, 0, 0)),
        pl.BlockSpec((1, SUBLANES, D_LRU), lambda b, t: (b, 0, 0)),
    ]
    args += [g.reshape(1, d), mod, w, pvec, wg, seed]
    out_specs = [
        pl.BlockSpec((1, ts, n), lambda b, t: (b, t, 0)),
        pl.BlockSpec((1, ts, D_LRU), lambda b, t: (b, t, 0)),
        pl.BlockSpec((1, ts, D_LRU), lambda b, t: (b, t, 0)),
        pl.BlockSpec((1, SUBLANES, D_LRU), lambda b, t: (b, 0, 0)),
    ]
    out_shape = [
        jax.ShapeDtypeStruct((nb, s, n), BF16),
        jax.ShapeDtypeStruct((nb, s, D_LRU), BF16),
        jax.ShapeDtypeStruct((nb, s, D_LRU), F32),
        jax.ShapeDtypeStruct((nb, SUBLANES, D_LRU), F32),
    ]
    if add_pos:
        out_specs.append(pl.BlockSpec((1, ts, d), lambda b, t: (b, t, 0)))
        out_shape.append(jax.ShapeDtypeStruct((nb, s, d), F32))
    res = pl.pallas_call(
        functools.partial(_in_fwd_kernel, ts=ts, add_pos=add_pos, mod_row=mod_row),
        grid=(nb, s // ts),
        in_specs=in_specs,
        out_specs=out_specs,
        out_shape=out_shape,
        scratch_shapes=[
            pltpu.VMEM((ts, D_LRU), F32),
            pltpu.VMEM((SUBLANES, D_LRU), F32),
        ],
        compiler_params=pltpu.CompilerParams(
            dimension_semantics=("arbitrary", "arbitrary"),
            vmem_limit_bytes=VMEM_LIMIT),
        name="in_forward",
    )(*args)
    u, hf, rc, state = res[:4]
    return u, hf, rc, state, (res[4] if add_pos else x)


def _bwd_kernel(*refs, ts, merge, mod_row):
    if merge:
        (rc_ref, pvec_ref, wg_ref, seed_ref,
         gate_ref, v_ref, vprev_ref, vnext_ref, hf_ref, x_ref, mod_ref,
         dww_ref, cvec_ref, wo_ref,
         xo_ref, state_ref,
         hs_ref, carry_ref, g_ref, cs_ref, sh_ref) = refs
    else:
        (rc_ref, pvec_ref, wg_ref, seed_ref,
         state_ref, hs_ref, carry_ref) = refs
    tg = pl.program_id(1)
    nt = pl.num_programs(1)
    t = nt - 1 - tg

    @pl.when(tg == 0)
    def _():
        carry_ref[...] = seed_ref[0]

    has_prev = (t > 0).astype(F32)
    has_next = (t < nt - 1).astype(F32)
    a, b = _lru_gates(rc_ref[0], pvec_ref, wg_ref)
    h = _lru_scan(a, b, carry_ref[...], hs_ref, ts, reverse=True)
    carry_ref[...] = h
    state_ref[0] = h
    if not merge:
        return

    y = hf_ref[0].astype(F32) + hs_ref[...]
    gate = gate_ref[0].astype(F32)
    cdf = 0.5 * (1.0 + jnp.tanh(math.sqrt(2.0 / math.pi) * (gate + 0.044715 * (gate * gate * gate))))
    lru = (y * (gate * cdf)).astype(BF16)

    def glu(vref):
        v = vref[0].astype(F32)
        return v[:, :D_CONV] * _sigmoid(v[:, D_CONV:])

    g_ref[0:HALO, :] = glu(vprev_ref) * has_prev
    g_ref[HALO:HALO + ts, :] = glu(v_ref)
    g_ref[HALO + ts:2 * HALO + ts, :] = glu(vnext_ref) * has_next
    pad = (CONF_CONV_W - 1) // 2
    dwb = cvec_ref[0:1, :]
    span = ts + 2 * HALO - SUBLANES
    for r in range(1, SUBLANES):
        sh_ref[r - 1, 0:span, :] = g_ref[pl.ds(r, span), :]

    def conv_chunk(i, carry):
        base = pl.multiple_of(i * CONV_CHUNK, CONV_CHUNK)
        groups = CONV_CHUNK // SUBLANES
        acc = jnp.zeros((groups, SUBLANES, D_CONV), F32) + dwb
        for k in range(CONF_CONV_W):
            q, r = divmod(HALO - pad + k, SUBLANES)
            rows = pl.ds(base + q * SUBLANES, CONV_CHUNK)
            tap = g_ref[rows, :] if r == 0 else sh_ref[r - 1, rows, :]
            wk = dww_ref[k * SUBLANES:(k + 1) * SUBLANES, :]
            acc = acc + wk[None] * tap.reshape(groups, SUBLANES, D_CONV)
        cs_ref[pl.ds(base, CONV_CHUNK), :] = acc.reshape(CONV_CHUNK, D_CONV)
        return carry

    lax.fori_loop(0, ts // CONV_CHUNK, conv_chunk, 0)
    cv = cs_ref[...]
    mu = jnp.mean(cv, axis=-1, keepdims=True)
    var = jnp.mean(jnp.square(cv - mu), axis=-1, keepdims=True)
    ln = (cv - mu) * lax.rsqrt(var + EPS) * cvec_ref[1:2, :] + cvec_ref[2:3, :]
    conf = _silu(ln).astype(BF16)

    o = jnp.dot(lru, wo_ref[0:D_LRU, :], preferred_element_type=F32)
    o = o + jnp.dot(conf, wo_ref[D_LRU:, :], preferred_element_type=F32)
    row = pl.program_id(0) if mod_row is None else mod_row
    xo_ref[0] = x_ref[0] + _mod_row(mod_ref, row, 2) * o


def _lru_backward(rc, pvec, wg, seed, *, ts, merge_args=None, mod_row=None):
    nb, s, _ = rc.shape
    nt = s // ts
    rev = lambda t: nt - 1 - t
    in_specs = [
        pl.BlockSpec((1, ts, D_LRU), lambda b, t: (b, rev(t), 0)),
        pl.BlockSpec((2 * SUBLANES, D_LRU), lambda b, t: (0, 0)),
        pl.BlockSpec((D_LRU // MXU_DIM, MXU_DIM, 2 * MXU_DIM), lambda b, t: (0, 0, 0)),
        pl.BlockSpec((1, SUBLANES, D_LRU), lambda b, t: (b, 0, 0)),
    ]
    args = [rc, pvec, wg, seed]
    state_spec = pl.BlockSpec((1, SUBLANES, D_LRU), lambda b, t: (b, 0, 0))
    state_shape = jax.ShapeDtypeStruct((nb, SUBLANES, D_LRU), F32)
    scratch = [
        pltpu.VMEM((ts, D_LRU), F32),
        pltpu.VMEM((SUBLANES, D_LRU), F32),
    ]
    merge = merge_args is not None
    if merge:
        u, hf, x, mod, dww, cvec, wo = merge_args
        d = x.shape[-1]
        in_specs += [pl.BlockSpec((1, ts, D_LRU), lambda b, t: (b, rev(t), 1))]
        in_specs += _halo_specs(2 * D_CONV, 1, ts, s, rev)
        in_specs += [
            pl.BlockSpec((1, ts, D_LRU), lambda b, t: (b, rev(t), 0)),
            pl.BlockSpec((1, ts, d), lambda b, t: (b, rev(t), 0)),
            pl.BlockSpec((SUBLANES, N_MOD * d), lambda b, t: (0, 0)),
            pl.BlockSpec((CONF_CONV_W * SUBLANES, D_CONV), lambda b, t: (0, 0)),
            pl.BlockSpec((SUBLANES, D_CONV), lambda b, t: (0, 0)),
            pl.BlockSpec((d, d), lambda b, t: (0, 0)),
        ]
        args += [u, u, u, u, hf, x, mod, dww, cvec, wo]
        out_specs = [pl.BlockSpec((1, ts, d), lambda b, t: (b, rev(t), 0)), state_spec]
        out_shape = [jax.ShapeDtypeStruct((nb, s, d), F32), state_shape]
        scratch += [
            pltpu.VMEM((ts + 2 * HALO, D_CONV), F32),
            pltpu.VMEM((ts, D_CONV), F32),
            pltpu.VMEM((SUBLANES - 1, ts + 2 * HALO, D_CONV), F32),
        ]
    else:
        out_specs = [state_spec]
        out_shape = [state_shape]
    res = pl.pallas_call(
        functools.partial(_bwd_kernel, ts=ts, merge=merge, mod_row=mod_row),
        grid=(nb, nt),
        in_specs=in_specs,
        out_specs=out_specs,
        out_shape=out_shape,
        scratch_shapes=scratch,
        compiler_params=pltpu.CompilerParams(
            dimension_semantics=("arbitrary", "arbitrary"),
            vmem_limit_bytes=VMEM_LIMIT),
        name="lru_backward_merge" if merge else "lru_backward",
    )(*args)
    return res if merge else (None, res[0])


ROUTE_BLOCK = 512
CHUNK = BF16_ROWS
BLOCK_CAP = 2 * ROUTE_BLOCK + LANES
CHUNKS_PER_BLOCK = BLOCK_CAP // CHUNK
MOE_TILE = 1024
FF_BLOCK = 6 * MXU_DIM
FF_SUB = MXU_DIM
CHUNKS_PER_TILE = MOE_TILE // CHUNK


def _route_kernel(x_ref, g_ref, mod_ref, wr_ref, hs_ref, ws_ref, cnt_ref, pos_ref,
                  *, blocks_per_batch, mod_row):
    i = pl.program_id(0)
    row = i // blocks_per_batch if mod_row is None else mod_row
    h = _rms_mod(x_ref[...], g_ref[...], _mod_row(mod_ref, row, 3), _mod_row(mod_ref, row, 4))
    wr = wr_ref[...]
    h_hi = h.astype(BF16)
    h_lo = (h - h_hi.astype(F32)).astype(BF16)
    w_hi = wr.astype(BF16)
    w_lo = (wr - w_hi.astype(F32)).astype(BF16)
    logits = (jnp.dot(h_hi, w_hi, preferred_element_type=F32)
              + jnp.dot(h_hi, w_lo, preferred_element_type=F32)
              + jnp.dot(h_lo, w_hi, preferred_element_type=F32))
    lane = lax.broadcasted_iota(jnp.int32, logits.shape, 1).astype(F32)
    neg = jnp.float32(-jnp.inf)
    l1 = jnp.where(lane < N_EXPERTS, logits, neg)
    m1 = jnp.max(l1, axis=-1, keepdims=True)
    i1 = jnp.min(jnp.where(l1 == m1, lane, float(LANES)), axis=-1, keepdims=True)
    l2 = jnp.where(lane == i1, neg, l1)
    m2 = jnp.max(l2, axis=-1, keepdims=True)
    i2 = jnp.min(jnp.where(l2 == m2, lane, float(LANES)), axis=-1, keepdims=True)
    e = jnp.exp(m2 - m1)
    p1 = 1.0 / (1.0 + e)
    p2 = e / (1.0 + e)

    sel1 = lane == i1
    sel2 = lane == i2
    member = jnp.where(sel1 | sel2, 1.0, 0.0)
    cnt = jnp.sum(member, axis=0, keepdims=True)
    padded = jnp.floor((cnt + (CHUNK - 1.0)) * (1.0 / CHUNK)) * CHUNK
    r128 = lax.broadcasted_iota(jnp.int32, (LANES, LANES), 0)
    c128 = lax.broadcasted_iota(jnp.int32, (LANES, LANES), 1)
    before = jnp.where(r128 < c128, 1.0, 0.0)
    seg_off = jnp.dot(jnp.broadcast_to(padded, (SUBLANES, LANES)), before,
                      preferred_element_type=F32, precision=lax.Precision.HIGHEST)[0:1, :]
    bt = member.shape[0]
    rt = lax.broadcasted_iota(jnp.int32, (bt, bt), 0)
    ct = lax.broadcasted_iota(jnp.int32, (bt, bt), 1)
    earlier = jnp.where(ct < rt, 1.0, 0.0).astype(BF16)
    rank = jnp.dot(earlier, member.astype(BF16), preferred_element_type=F32)
    slot_of = seg_off + rank
    pos1 = jnp.sum(jnp.where(sel1, slot_of, 0.0), axis=-1, keepdims=True)
    pos2 = jnp.sum(jnp.where(sel2, slot_of, 0.0), axis=-1, keepdims=True)
    stats = (jnp.where(lane == 0.0, pos1, 0.0) + jnp.where(lane == 1.0, pos2, 0.0)
             + jnp.where(lane == 2.0, p1, 0.0) + jnp.where(lane == 3.0, p2, 0.0))
    pos_ref[...] = stats
    cnt_ref[0] = jnp.broadcast_to(cnt, (SUBLANES, LANES))

    st = stats.T
    slot = lax.broadcasted_iota(jnp.int32, (BLOCK_CAP, bt), 0).astype(F32)
    hit1 = slot == st[0:1, :]
    hit2 = slot == st[1:2, :]
    perm = jnp.where(hit1 | hit2, 1.0, 0.0).astype(BF16)
    hs_ref[...] = jnp.dot(perm, h.astype(BF16), preferred_element_type=F32).astype(BF16)
    wrow = jnp.where(hit1, st[2:3, :], 0.0) + jnp.where(hit2, st[3:4, :], 0.0)
    ws_ref[...] = jnp.broadcast_to(jnp.sum(wrow, axis=-1, keepdims=True), (BLOCK_CAP, LANES))


def _route(xall, g, mod, wr, *, mod_row, blocks_per_batch):
    t, d = xall.shape
    nblk = t // ROUTE_BLOCK
    return pl.pallas_call(
        functools.partial(_route_kernel, blocks_per_batch=blocks_per_batch,
                          mod_row=mod_row),
        grid=(nblk,),
        in_specs=[
            pl.BlockSpec((ROUTE_BLOCK, d), lambda i: (i, 0)),
            pl.BlockSpec((1, d), lambda i: (0, 0)),
            pl.BlockSpec((SUBLANES, N_MOD * d), lambda i: (0, 0)),
            pl.BlockSpec((d, LANES), lambda i: (0, 0)),
        ],
        out_specs=[
            pl.BlockSpec((BLOCK_CAP, d), lambda i: (i, 0)),
            pl.BlockSpec((BLOCK_CAP, LANES), lambda i: (i, 0)),
            pl.BlockSpec((1, SUBLANES, LANES), lambda i: (i, 0, 0)),
            pl.BlockSpec((ROUTE_BLOCK, LANES), lambda i: (i, 0)),
        ],
        out_shape=[
            jax.ShapeDtypeStruct((nblk * BLOCK_CAP, d), BF16),
            jax.ShapeDtypeStruct((nblk * BLOCK_CAP, LANES), F32),
            jax.ShapeDtypeStruct((nblk, SUBLANES, LANES), F32),
            jax.ShapeDtypeStruct((t, LANES), F32),
        ],
        compiler_params=pltpu.CompilerParams(
            dimension_semantics=("arbitrary",),
            vmem_limit_bytes=VMEM_LIMIT),
        name="route",
    )(xall, g.reshape(1, d), mod, wr)


def _route_tables(cnt, n_tiles_max):
    nblk = cnt.shape[0]
    n_be = (cnt[:, 0, :N_EXPERTS].astype(jnp.int32) + (CHUNK - 1)) // CHUNK
    ends_be = jnp.cumsum(n_be, axis=1)
    off_be = ends_be - n_be
    start_be = jnp.cumsum(n_be, axis=0) - n_be
    g_e = jnp.sum(n_be, axis=0)
    tiles_e = (g_e + (CHUNKS_PER_TILE - 1)) // CHUNKS_PER_TILE
    tend_e = jnp.cumsum(tiles_e)
    tstart_e = tend_e - tiles_e
    n_tiles = tend_e[-1]

    experts = jnp.arange(N_EXPERTS, dtype=jnp.int32)

    def pick(onehot, table):
        return jnp.sum(jnp.where(onehot, table, 0), axis=-1)

    c = jnp.arange(CHUNKS_PER_BLOCK, dtype=jnp.int32)
    e_bc = jnp.sum((c[None, :, None] >= ends_be[:, None, :]).astype(jnp.int32), axis=-1)
    oh_bc = e_bc[:, :, None] == experts
    shift_be = tstart_e[None, :] * CHUNKS_PER_TILE + start_be - off_be
    d_bc = pick(oh_bc, shift_be[:, None, :]) + c[None, :]
    dst = jnp.where(e_bc < N_EXPERTS, d_bc, -1).reshape(-1).astype(jnp.int32)

    ti = jnp.arange(n_tiles_max, dtype=jnp.int32)
    ti_cl = jnp.minimum(ti, n_tiles - 1)
    tile_expert = jnp.minimum(
        jnp.sum((ti_cl[:, None] >= tend_e[None, :]).astype(jnp.int32), axis=-1), N_EXPERTS - 1)
    oh_t = tile_expert[:, None] == experts
    first_q = (ti - pick(oh_t, tstart_e[None, :])) * CHUNKS_PER_TILE

    q = first_q[:, None] + jnp.arange(CHUNKS_PER_TILE, dtype=jnp.int32)[None, :]
    cum_tb = pick(oh_t[:, None, :], (start_be + n_be)[None, :, :])
    b_tj = jnp.minimum(jnp.sum((q[:, :, None] >= cum_tb[:, None, :]).astype(jnp.int32), axis=-1),
                       nblk - 1)
    oh_b = b_tj[:, :, None] == jnp.arange(nblk, dtype=jnp.int32)
    local_tb = pick(oh_t[:, None, :], (off_be - start_be)[None, :, :])
    src = b_tj * CHUNKS_PER_BLOCK + q + pick(oh_b, local_tb[:, None, :])
    src = jnp.clip(src, 0, nblk * CHUNKS_PER_BLOCK - 1).reshape(-1).astype(jnp.int32)
    return tile_expert.astype(jnp.int32), src, n_tiles.reshape(1).astype(jnp.int32), dst


DMA_UNROLL = 8


def _start_chunk_gather(table_ref, base, sources, dst_buf, sem):
    n_slots = dst_buf.shape[0] // CHUNK

    def step(j, carry):
        cid = jnp.maximum(table_ref[base + j], 0)
        dst = dst_buf.at[pl.ds(pl.multiple_of(j * CHUNK, CHUNK), CHUNK), :]
        lo = 0
        for ref, n in sources:
            def start(ref=ref, lo=lo):
                src = ref.at[pl.ds(pl.multiple_of((cid - lo) * CHUNK, CHUNK), CHUNK), :]
                pltpu.make_async_copy(src, dst, sem).start()
            if len(sources) == 1:
                start()
            else:
                pl.when((cid >= lo) & (cid < lo + n))(start)
            lo += n
        return carry

    lax.fori_loop(0, n_slots, step, 0, unroll=DMA_UNROLL)


def _wait_chunk_gather(sources, dst_buf, sem):
    ref = sources[0][0]
    pltpu.make_async_copy(ref.at[pl.ds(0, dst_buf.shape[0]), :], dst_buf, sem).wait()


def _swiglu_accumulate(load_x, w1_ref, w3_ref, w2_ref, acc_ref, f, ff):
    last_valid = (ff - (pl.cdiv(ff, FF_BLOCK) - 1) * FF_BLOCK) // FF_SUB
    is_last = f == pl.cdiv(ff, FF_BLOCK) - 1
    for c in range(FF_BLOCK // FF_SUB):
        cols = slice(c * FF_SUB, (c + 1) * FF_SUB)

        def chunk(cols=cols):
            xb = load_x()
            up = jnp.dot(xb, w1_ref[:, cols].astype(BF16), preferred_element_type=F32)
            lin = jnp.dot(xb, w3_ref[:, cols].astype(BF16), preferred_element_type=F32)
            hid = (_silu(up) * lin).astype(BF16)
            acc_ref[...] += jnp.dot(hid, w2_ref[cols, :].astype(BF16), preferred_element_type=F32)

        if c < last_valid:
            chunk()
        else:
            pl.when(jnp.logical_not(is_last))(chunk)


def _moe_kernel(te_ref, src_ref, nt_ref, *refs, ff, src_chunks):
    n_src = len(src_chunks)
    hs_hbm = tuple(zip(refs[:n_src], src_chunks))
    ws_hbm = tuple(zip(refs[n_src:2 * n_src], src_chunks))
    w1_ref, w3_ref, w2_ref, ys_ref, xbuf, wbuf, acc_ref, sems = refs[2 * n_src:]
    i = pl.program_id(0)
    f = pl.program_id(1)
    n_tiles = nt_ref[0]
    valid = i < n_tiles
    slot = lax.rem(i, 2)

    def start_gather(tile, buf_slot):
        base = tile * CHUNKS_PER_TILE
        _start_chunk_gather(src_ref, base, hs_hbm, xbuf.at[buf_slot], sems.at[0, buf_slot])
        _start_chunk_gather(src_ref, base, ws_hbm, wbuf.at[buf_slot], sems.at[1, buf_slot])

    @pl.when((i == 0) & (f == 0))
    def _():
        start_gather(0, 0)

    @pl.when(valid & (f == 0))
    def _():
        _wait_chunk_gather(hs_hbm, xbuf.at[slot], sems.at[0, slot])
        _wait_chunk_gather(ws_hbm, wbuf.at[slot], sems.at[1, slot])

        @pl.when(i + 1 < n_tiles)
        def _():
            start_gather(i + 1, 1 - slot)

        acc_ref[...] = jnp.zeros_like(acc_ref)

    @pl.when(valid)
    def _():
        _swiglu_accumulate(lambda: xbuf[slot], w1_ref.at[0, 0], w3_ref.at[0, 0], w2_ref.at[0, 0],
                           acc_ref, f, ff)

    @pl.when(valid & (f == pl.num_programs(1) - 1))
    def _():
        ys_ref[...] = (acc_ref[...] * wbuf[slot][:, 0:1]).astype(BF16)

    @pl.when(jnp.logical_not(valid) & (f == 0))
    def _():
        ys_ref[...] = jnp.zeros_like(ys_ref)


def _moe_grouped(hs_list, ws_list, tables, w1, w3, w2, layer, *, n_tiles_max):
    tile_expert, src, n_tiles, _ = tables
    d = hs_list[0].shape[1]
    ff = w1.shape[-1]
    tf = FF_BLOCK
    nf = pl.cdiv(ff, tf)
    src_chunks = tuple(h.shape[0] // CHUNK for h in hs_list)
    any_spec = pl.BlockSpec(memory_space=pl.ANY)

    def w_in_map(i, f, te, sr, nt):
        return (layer, te[i], 0, jnp.where(i < nt[0], f, nf - 1))

    def w_out_map(i, f, te, sr, nt):
        return (layer, te[i], jnp.where(i < nt[0], f, nf - 1), 0)

    def y_map(i, f, te, sr, nt):
        return (i, 0)

    grid_spec = pltpu.PrefetchScalarGridSpec(
        num_scalar_prefetch=3,
        grid=(n_tiles_max, nf),
        in_specs=[any_spec] * (2 * len(hs_list)) + [
            pl.BlockSpec((1, 1, d, tf), w_in_map),
            pl.BlockSpec((1, 1, d, tf), w_in_map),
            pl.BlockSpec((1, 1, tf, d), w_out_map),
        ],
        out_specs=pl.BlockSpec((MOE_TILE, d), y_map),
        scratch_shapes=[
            pltpu.VMEM((2, MOE_TILE, d), BF16),
            pltpu.VMEM((2, MOE_TILE, LANES), F32),
            pltpu.VMEM((MOE_TILE, d), F32),
            pltpu.SemaphoreType.DMA((2, 2)),
        ],
    )
    return pl.pallas_call(
        functools.partial(_moe_kernel, ff=ff, src_chunks=src_chunks),
        grid_spec=grid_spec,
        out_shape=jax.ShapeDtypeStruct((n_tiles_max * MOE_TILE, d), BF16),
        compiler_params=pltpu.CompilerParams(
            dimension_semantics=("arbitrary", "arbitrary"),
            vmem_limit_bytes=VMEM_LIMIT),
        name="moe_grouped",
    )(tile_expert, src, n_tiles, *hs_list, *ws_list, w1, w3, w2)


def _combine_kernel(dst_ref, ys_hbm, pos_ref, x_ref, fg_ref, mod_ref, o_ref, ybuf, sems,
                    *, blocks_per_batch, mod_row, n_src_chunks, final):
    i = pl.program_id(0)
    row = i // blocks_per_batch if mod_row is None else mod_row
    buf = lax.rem(i, 2)

    source = ((ys_hbm, n_src_chunks),)

    def start_gather(blk, buf_slot):
        _start_chunk_gather(dst_ref, blk * CHUNKS_PER_BLOCK, source, ybuf.at[buf_slot], sems.at[buf_slot])

    @pl.when(i == 0)
    def _():
        start_gather(0, 0)

    @pl.when(i + 1 < pl.num_programs(0))
    def _():
        start_gather(i + 1, 1 - buf)

    stats = pos_ref[...]
    slot = lax.broadcasted_iota(jnp.int32, (stats.shape[0], BLOCK_CAP), 1).astype(F32)
    pick = jnp.where((slot == stats[:, 0:1]) | (slot == stats[:, 1:2]), 1.0, 0.0).astype(BF16)
    _wait_chunk_gather(source, ybuf.at[buf], sems.at[buf])
    moe = jnp.dot(pick, ybuf[buf], preferred_element_type=F32)
    y = x_ref[...] + _mod_row(mod_ref, row, 5) * moe
    if final:
        ms = jnp.mean(y * y, axis=-1, keepdims=True)
        y = y * lax.rsqrt(ms + EPS) * fg_ref[...]
    o_ref[...] = y


def _combine(ys, dst, pos, xall, fg, mod, *, mod_row, blocks_per_batch, final):
    t, d = xall.shape
    nblk = t // ROUTE_BLOCK
    grid_spec = pltpu.PrefetchScalarGridSpec(
        num_scalar_prefetch=1,
        grid=(nblk,),
        in_specs=[
            pl.BlockSpec(memory_space=pl.ANY),
            pl.BlockSpec((ROUTE_BLOCK, LANES), lambda i, ds: (i, 0)),
            pl.BlockSpec((ROUTE_BLOCK, d), lambda i, ds: (i, 0)),
            pl.BlockSpec((1, d), lambda i, ds: (0, 0)),
            pl.BlockSpec((SUBLANES, N_MOD * d), lambda i, ds: (0, 0)),
        ],
        out_specs=pl.BlockSpec((ROUTE_BLOCK, d), lambda i, ds: (i, 0)),
        scratch_shapes=[
            pltpu.VMEM((2, BLOCK_CAP, d), BF16),
            pltpu.SemaphoreType.DMA((2,)),
        ],
    )
    return pl.pallas_call(
        functools.partial(_combine_kernel, blocks_per_batch=blocks_per_batch, mod_row=mod_row,
                          n_src_chunks=ys.shape[0] // CHUNK, final=final),
        grid_spec=grid_spec,
        out_shape=jax.ShapeDtypeStruct((t, d), F32),
        compiler_params=pltpu.CompilerParams(
            dimension_semantics=("arbitrary",),
            vmem_limit_bytes=VMEM_LIMIT),
        name="moe_combine",
    )(dst, ys, pos, xall, fg.reshape(1, d), mod)


def _ffn_kernel(x_ref, g_ref, mod_ref, w1_ref, w3_ref, w2_ref, o_ref, h_ref, acc_ref,
                *, tiles_per_batch, mod_row, ff):
    i = pl.program_id(0)
    f = pl.program_id(1)
    row = i // tiles_per_batch if mod_row is None else mod_row

    @pl.when(f == 0)
    def _():
        h = _rms_mod(x_ref[...], g_ref[...], _mod_row(mod_ref, row, 3), _mod_row(mod_ref, row, 4))
        h_ref[...] = h.astype(BF16)
        acc_ref[...] = jnp.zeros_like(acc_ref)

    _swiglu_accumulate(lambda: h_ref[...], w1_ref.at[0], w3_ref.at[0], w2_ref.at[0], acc_ref, f, ff)

    @pl.when(f == pl.num_programs(1) - 1)
    def _():
        o_ref[...] = x_ref[...] + _mod_row(mod_ref, row, 5) * acc_ref[...]


def _ffn(x2, g, mod, w1, w3, w2, layer, *, tm, tiles_per_batch, mod_row):
    t, d = x2.shape
    ff = w1.shape[-1]
    tf = FF_BLOCK
    return pl.pallas_call(
        functools.partial(_ffn_kernel, tiles_per_batch=tiles_per_batch, mod_row=mod_row, ff=ff),
        grid=(t // tm, pl.cdiv(ff, tf)),
        in_specs=[
            pl.BlockSpec((tm, d), lambda i, f: (i, 0)),
            pl.BlockSpec((1, d), lambda i, f: (0, 0)),
            pl.BlockSpec((SUBLANES, N_MOD * d), lambda i, f: (0, 0)),
            pl.BlockSpec((1, d, tf), lambda i, f: (layer, 0, f)),
            pl.BlockSpec((1, d, tf), lambda i, f: (layer, 0, f)),
            pl.BlockSpec((1, tf, d), lambda i, f: (layer, f, 0)),
        ],
        out_specs=pl.BlockSpec((tm, d), lambda i, f: (i, 0)),
        out_shape=jax.ShapeDtypeStruct((t, d), F32),
        scratch_shapes=[pltpu.VMEM((tm, d), BF16), pltpu.VMEM((tm, d), F32)],
        compiler_params=pltpu.CompilerParams(
            dimension_semantics=("arbitrary", "arbitrary"),
            vmem_limit_bytes=VMEM_LIMIT),
        name="dense_ffn",
    )(x2, g.reshape(1, d), mod, w1, w3, w2)


def _sincos_2d(rows, cols, dim):
    quarter = dim // 4
    omega = 1.0 / (10000.0 ** (jnp.arange(quarter, dtype=F32) / quarter))

    def emb1d(n):
        ang = jnp.arange(n, dtype=F32)[:, None] * omega[None, :]
        return jnp.concatenate([jnp.sin(ang), jnp.cos(ang)], axis=-1)

    er = jnp.broadcast_to(emb1d(rows)[:, None, :], (rows, cols, dim // 2))
    ec = jnp.broadcast_to(emb1d(cols)[None, :, :], (rows, cols, dim // 2))
    return jnp.concatenate([er, ec], axis=-1).reshape(rows * cols, dim)


def _block_diag_tiles(w):
    per = MXU_DIM // LRU_HEAD_DIM
    w4 = w.reshape(D_LRU // MXU_DIM, per, LRU_HEAD_DIM, LRU_HEAD_DIM)
    eye = jnp.eye(per, dtype=w.dtype)
    return jnp.einsum("ghij,hk->ghikj", w4, eye).reshape(D_LRU // MXU_DIM, MXU_DIM, MXU_DIM)


def _gate_weights(wa, wx):
    return jnp.concatenate([_block_diag_tiles(wa), _block_diag_tiles(wx)], axis=-1).astype(BF16)


def _lru_vectors(conv_w, conv_b, ba, bx, lam):
    rows = jnp.concatenate([conv_w, conv_b[None], ba[None], bx[None], lam[None]], axis=0)
    return jnp.pad(rows, ((0, 2 * SUBLANES - rows.shape[0]), (0, 0)))


def kernel(x, c, ctx, c_ctx, w_mod, b_mod, norm1_g, norm2_g, w_in, w_out, lru_conv_w, lru_conv_b, lru_wa, lru_ba, lru_wx, lru_bx, lru_lambda, conf_dw_w, conf_dw_b, conf_ln_g, conf_ln_b, ffn_w1, ffn_w3, ffn_w2, moe_router, moe_w1, moe_w3, moe_w2, final_g):
    nb, s, d = x.shape
    n_ctx = ctx.shape[1]
    pos = _sincos_2d(s // GRID_W, GRID_W, d)
    cc = jnp.concatenate([c, c_ctx[None], jnp.zeros((SUBLANES - nb - 1, d), F32)], axis=0)
    mod_all = _modulation(cc, w_mod, b_mod)

    ts_x, ts_c = 512, n_ctx
    tm_x = 1024
    tiles_per_batch = s // tm_x
    zero_state = jnp.zeros((nb, SUBLANES, D_LRU), F32)
    xc = ctx.reshape(1, nb * n_ctx, d)

    ffn_w1_b, ffn_w3_b, ffn_w2_b = (w.astype(BF16) for w in (ffn_w1, ffn_w3, ffn_w2))

    for l in range(DEPTH):
        last = l == DEPTH - 1
        mod = mod_all[l]
        w_in_b = w_in[l].astype(BF16)
        w_out_b = w_out[l].astype(BF16)
        pvec = [_lru_vectors(lru_conv_w[l], lru_conv_b[l], lru_ba[l, dr], lru_bx[l, dr],
                             lru_lambda[l, dr]) for dr in range(2)]
        wg = [_gate_weights(lru_wa[l, dr], lru_wx[l, dr]) for dr in range(2)]
        dww = jnp.repeat(conf_dw_w[l], SUBLANES, axis=0)
        cvec = jnp.pad(jnp.stack([conf_dw_b[l], conf_ln_g[l], conf_ln_b[l]]),
                       ((0, SUBLANES - 3), (0, 0)))

        u_c, hf_c, rc_c, st_f, _ = _in_forward(
            xc.reshape(nb, n_ctx, d), None, norm1_g[l], mod, w_in_b[:, :D_LRU] if last else w_in_b,
            pvec[0], wg[0], zero_state, ts=ts_c, mod_row=CTX_MOD_ROW)
        u_x, hf_x, rc_x, _, x = _in_forward(
            x, pos if l == 0 else None, norm1_g[l], mod, w_in_b, pvec[0], wg[0], st_f,
            ts=ts_x, mod_row=None)
        if last:
            _, st_b = _lru_backward(rc_c, pvec[1], wg[1], zero_state, ts=ts_c)
        else:
            xc_new, st_b = _lru_backward(
                rc_c, pvec[1], wg[1], zero_state, ts=ts_c,
                merge_args=(u_c, hf_c, xc.reshape(nb, n_ctx, d), mod, dww, cvec, w_out_b),
                mod_row=CTX_MOD_ROW)
            xc = xc_new.reshape(1, nb * n_ctx, d)
        x, _ = _lru_backward(rc_x, pvec[1], wg[1], st_b, ts=ts_x,
                             merge_args=(u_x, hf_x, x, mod, dww, cvec, w_out_b), mod_row=None)

        j = l // 2
        x2 = x.reshape(nb * s, d)
        if l % 2 == 0:
            x2 = _ffn(x2, norm2_g[l], mod, ffn_w1_b, ffn_w3_b, ffn_w2_b, j, tm=tm_x,
                      tiles_per_batch=tiles_per_batch, mod_row=None)
            xc2 = _ffn(xc[0], norm2_g[l], mod, ffn_w1_b, ffn_w3_b, ffn_w2_b, j, tm=512,
                       tiles_per_batch=1, mod_row=CTX_MOD_ROW)
            xc = xc2.reshape(1, nb * n_ctx, d)
        else:
            groups = [(x2, None)] if last else [(x2, None), (xc[0], CTX_MOD_ROW)]
            blocks_per_batch = s // ROUTE_BLOCK
            wr = jnp.pad(moe_router[j], ((0, 0), (0, LANES - N_EXPERTS)))
            routed = [_route(xg, norm2_g[l], mod, wr, mod_row=row, blocks_per_batch=blocks_per_batch)
                      for xg, row in groups]
            cnt = jnp.concatenate([r[2] for r in routed], axis=0)
            nblk = cnt.shape[0]
            n_tiles_max = (nblk * CHUNKS_PER_BLOCK + N_EXPERTS * (CHUNKS_PER_TILE - 1)) // CHUNKS_PER_TILE
            tables = _route_tables(cnt, n_tiles_max)
            ys = _moe_grouped([r[0] for r in routed], [r[1] for r in routed], tables,
                              moe_w1, moe_w3, moe_w2, j, n_tiles_max=n_tiles_max)
            outs, first = [], 0
            for (xg, row), r in zip(groups, routed):
                n_chunks = xg.shape[0] // ROUTE_BLOCK * CHUNKS_PER_BLOCK
                outs.append(_combine(ys, tables[3][first:first + n_chunks], r[3], xg, final_g, mod,
                                     mod_row=row, blocks_per_batch=blocks_per_batch, final=last))
                first += n_chunks
            x2 = outs[0]
            if not last:
                xc = outs[1].reshape(1, nb * n_ctx, d)
        x = x2.reshape(nb, s, d)
    return x
```
